```python
import jax, jax.numpy as jnp
from jax import lax
import numpy as np

D_MODEL = 1024
BATCH = 16
SEQ = 2048
DEPTH = 2

CHUNK = 64
M_HEADS = 4
M_HEAD_DIM = 256
M_WIDTH = M_HEADS * M_HEAD_DIM
M_CONV = 4
A_HEADS = 16
A_KV_HEADS = 4
A_HEAD_DIM = 64
A_Q_WIDTH = A_HEADS * A_HEAD_DIM
A_KV_WIDTH = A_KV_HEADS * A_HEAD_DIM
WINDOW = 128
A_PREV_CHUNKS = -(-(WINDOW - 1) // CHUNK)
ROPE_THETA = 10000.0
D_FF = 2816
N_IN = 3 * M_WIDTH + 2 * M_HEADS + A_Q_WIDTH + 2 * A_KV_WIDTH
EPS = 1e-6

kernel_name = 'hybrid_mlstm_swa_macaron_adaln'


def rms_norm(x, w):
    xf = x.astype(jnp.float32)
    y = xf * lax.rsqrt(jnp.mean(xf * xf, axis=-1, keepdims=True) + EPS)
    return (y * w.astype(jnp.float32)).astype(x.dtype)


def modulate(x, norm_w, shift, scale):
    return rms_norm(x, norm_w) * (1 + scale) + shift


def swiglu(h, w_in, w_out):
    a, g = jnp.split(h @ w_in, 2, axis=-1)
    return (jax.nn.silu(a) * g) @ w_out


def rope(x, pos):
    d = x.shape[-1]
    inv_freq = ROPE_THETA ** (-jnp.arange(0, d, 2, dtype=jnp.float32) / d)
    ang = pos.astype(jnp.float32)[..., None] * inv_freq
    cos, sin = jnp.cos(ang)[:, :, None, :], jnp.sin(ang)[:, :, None, :]
    xf = x.astype(jnp.float32)
    x1, x2 = xf[..., : d // 2], xf[..., d // 2:]
    return jnp.concatenate([x1 * cos - x2 * sin, x2 * cos + x1 * sin], axis=-1).astype(x.dtype)


def _mlstm_chunk(carry, xs):
    C, n, m = carry
    q, k, v, li, lf = xs
    L = q.shape[2]
    b = jnp.cumsum(lf, axis=-1)
    causal = jnp.arange(L)[:, None] >= jnp.arange(L)[None, :]
    d = jnp.where(causal, b[..., :, None] - b[..., None, :] + li[..., None, :], -jnp.inf)
    a = b + m[..., None]
    m_t = jnp.maximum(a, jnp.max(d, axis=-1))
    w_intra = jnp.exp(d - m_t[..., None])
    w_inter = jnp.exp(a - m_t)
    s = jnp.einsum('bhtd,bhsd->bhts', q, k) * w_intra
    num = jnp.einsum('bhts,bhsv->bhtv', s, v) + w_inter[..., None] * jnp.einsum('bhtd,bhdv->bhtv', q, C)
    den = jnp.sum(s, axis=-1) + w_inter * jnp.einsum('bhtd,bhd->bht', q, n)
    h = num / jnp.maximum(jnp.abs(den), jnp.exp(-m_t))[..., None]
    g = b[..., -1:] - b + li
    a_end = b[..., -1] + m
    m_new = jnp.maximum(a_end, jnp.max(g, axis=-1))
    wg = jnp.exp(g - m_new[..., None])
    decay = jnp.exp(a_end - m_new)
    kw = k * wg[..., None]
    C_new = decay[..., None, None] * C + jnp.einsum('bhsd,bhsv->bhdv', kw, v)
    n_new = decay[..., None] * n + jnp.sum(kw, axis=2)
    return (C_new, n_new, m_new), h


def mlstm_branch(u, v, o_pre, i_pre, f_pre, conv_w, conv_b, wq, wk, out_norm, skip):
    B, S, _ = u.shape
    nc = S // CHUNK
    uc = lax.conv_general_dilated(u, conv_w[:, None, :], window_strides=(1,),
                                  padding=[(M_CONV - 1, 0)],
                                  dimension_numbers=('NWC', 'WIO', 'NWC'),
                                  feature_group_count=M_WIDTH)
    ua = jax.nn.silu(uc + conv_b)
    uh = ua.reshape(B, S, M_HEADS, M_HEAD_DIM)
    q = jnp.einsum('bshd,hde->bshe', uh, wq)
    k = jnp.einsum('bshd,hde->bshe', uh, wk) * (M_HEAD_DIM ** -0.5)
    vh = v.reshape(B, S, M_HEADS, M_HEAD_DIM)

    def chunks4(t):
        return t.astype(jnp.float32).reshape(B, nc, CHUNK, M_HEADS, -1).transpose(1, 0, 3, 2, 4)

    def chunks3(t):
        return t.astype(jnp.float32).reshape(B, nc, CHUNK, M_HEADS).transpose(1, 0, 3, 2)

    li = chunks3(i_pre)
    lf = chunks3(jax.nn.log_sigmoid(f_pre.astype(jnp.float32)))
    carry0 = (jnp.zeros((B, M_HEADS, M_HEAD_DIM, M_HEAD_DIM), jnp.float32),
              jnp.zeros((B, M_HEADS, M_HEAD_DIM), jnp.float32),
              jnp.zeros((B, M_HEADS), jnp.float32))
    _, hs = lax.scan(_mlstm_chunk, carry0, (chunks4(q), chunks4(k), chunks4(vh), li, lf))
    h = hs.transpose(1, 0, 3, 2, 4).reshape(B, S, M_HEADS, M_HEAD_DIM)
    h = h * lax.rsqrt(jnp.mean(h * h, axis=-1, keepdims=True) + EPS)
    h = (h.reshape(B, S, M_WIDTH) * out_norm.astype(jnp.float32)).astype(u.dtype)
    return jax.nn.sigmoid(o_pre) * (h + skip * ua)


def swa_branch(q, k, v, pos, q_norm, k_norm, sinks):
    B, S, _ = q.shape
    nc = S // CHUNK
    G = A_HEADS // A_KV_HEADS
    q = rope(rms_norm(q.reshape(B, S, A_HEADS, A_HEAD_DIM), q_norm), pos)
    k = rope(rms_norm(k.reshape(B, S, A_KV_HEADS, A_HEAD_DIM), k_norm), pos)
    v = v.reshape(B, S, A_KV_HEADS, A_HEAD_DIM)
    qb = q.reshape(B, nc, CHUNK, A_KV_HEADS, G, A_HEAD_DIM)

    def band(t):
        t = t.reshape(B, nc, CHUNK, A_KV_HEADS, A_HEAD_DIM)
        tp = jnp.pad(t, ((0, 0), (A_PREV_CHUNKS, 0), (0, 0), (0, 0), (0, 0)))
        return jnp.concatenate([tp[:, j:j + nc] for j in range(A_PREV_CHUNKS + 1)], axis=2)

    kb, vb = band(k), band(v)
    key_chunk = jnp.arange(nc)[:, None] - A_PREV_CHUNKS + jnp.arange(A_PREV_CHUNKS + 1)[None, :]
    valid = jnp.repeat(key_chunk >= 0, CHUNK, axis=1)
    s = jnp.einsum('bnqhgd,bnkhd->bnhgqk', qb, kb).astype(jnp.float32) * (A_HEAD_DIM ** -0.5)
    s = jnp.where(valid[None, :, None, None, None, :], s, -jnp.inf)
    sink = jnp.broadcast_to(sinks.astype(jnp.float32).reshape(1, 1, A_KV_HEADS, G, 1, 1),
                            s.shape[:-1] + (1,))
    p = jax.nn.softmax(jnp.concatenate([s, sink], axis=-1), axis=-1)[..., :-1]
    o = jnp.einsum('bnhgqk,bnkhd->bnqhgd', p.astype(v.dtype), vb)
    return o.reshape(B, S, A_Q_WIDTH)


def hybrid_mixer(h, pos, w_in, m_gate_b, m_conv_w, m_conv_b, m_wq, m_wk, m_out_norm, m_skip,
                 a_q_norm, a_k_norm, a_sinks, proj_a, proj_b, merge_w, merge_b, w_out):
    sizes = [M_WIDTH, M_WIDTH, M_WIDTH, M_HEADS, M_HEADS, A_Q_WIDTH, A_KV_WIDTH, A_KV_WIDTH]
    offs = [int(o) for o in np.cumsum(sizes)[:-1]]
    u, vm, om, im, fm, qa, ka, va = jnp.split(h @ w_in, offs, axis=-1)
    im = im + m_gate_b[:M_HEADS]
    fm = fm + m_gate_b[M_HEADS:]
    ya = mlstm_branch(u, vm, om, im, fm, m_conv_w, m_conv_b, m_wq, m_wk, m_out_norm, m_skip)
    yb = swa_branch(qa, ka, va, pos, a_q_norm, a_k_norm, a_sinks)
    ga, gb = jnp.split(jax.nn.sigmoid(h @ merge_w + merge_b), 2, axis=-1)
    merged = ga * (ya @ proj_a) + gb * (yb @ proj_b)
    return merged @ w_out


def setup_inputs(seed: int = 0) -> dict:
    key = jax.random.key(seed)
    ks = jax.random.split(key, 32)
    L, D, F = DEPTH, D_MODEL, D_FF
    nrm = jax.random.normal

    def gain(k, shape):
        return 1.0 + 0.02 * nrm(k, shape, jnp.float32)

    offsets = jax.random.randint(ks[2], (BATCH,), 0, 64) * CHUNK
    positions = (offsets[:, None] + jnp.arange(SEQ, dtype=jnp.int32)[None, :]).astype(jnp.int32)
    m_gate_b = jnp.concatenate([0.1 * nrm(ks[8], (L, M_HEADS), jnp.float32),
                                jax.random.uniform(ks[9], (L, M_HEADS), jnp.float32, 3.0, 6.0)], axis=-1)
    return {
        'x': nrm(ks[0], (BATCH, SEQ, D), jnp.float32),
        'c': nrm(ks[1], (BATCH, D), jnp.float32),
        'positions': positions,
        'ada_w': 0.5 * D ** -0.5 * nrm(ks[3], (L, D, 9 * D), jnp.float32),
        'ada_b': 0.01 * nrm(ks[4], (L, 9 * D), jnp.float32),
        'ffn1_norm': gain(ks[5], (L, D)),
        'ffn1_w_in': D ** -0.5 * nrm(ks[6], (L, D, 2 * F), jnp.float32),
        'ffn1_w_out': F ** -0.5 * nrm(ks[7], (L, F, D), jnp.float32),
        'mix_norm': gain(ks[10], (L, D)),
        'mix_w_in': D ** -0.5 * nrm(ks[11], (L, D, N_IN), jnp.float32),
        'm_gate_b': m_gate_b,
        'm_conv_w': M_CONV ** -0.5 * nrm(ks[12], (L, M_CONV, M_WIDTH), jnp.float32),
        'm_conv_b': 0.01 * nrm(ks[13], (L, M_WIDTH), jnp.float32),
        'm_wq': M_HEAD_DIM ** -0.5 * nrm(ks[14], (L, M_HEADS, M_HEAD_DIM, M_HEAD_DIM), jnp.float32),
        'm_wk': M_HEAD_DIM ** -0.5 * nrm(ks[15], (L, M_HEADS, M_HEAD_DIM, M_HEAD_DIM), jnp.float32),
        'm_out_norm': gain(ks[16], (L, M_WIDTH)),
        'm_skip': gain(ks[17], (L, M_WIDTH)),
        'a_q_norm': gain(ks[18], (L, A_HEAD_DIM)),
        'a_k_norm': gain(ks[19], (L, A_HEAD_DIM)),
        'a_sinks': nrm(ks[20], (L, A_HEADS), jnp.float32),
        'proj_a': M_WIDTH ** -0.5 * nrm(ks[21], (L, M_WIDTH, D), jnp.float32),
        'proj_b': A_Q_WIDTH ** -0.5 * nrm(ks[22], (L, A_Q_WIDTH, D), jnp.float32),
        'merge_w': D ** -0.5 * nrm(ks[23], (L, D, 2 * D), jnp.float32),
        'merge_b': 0.01 * nrm(ks[24], (L, 2 * D), jnp.float32),
        'w_out': D ** -0.5 * nrm(ks[25], (L, D, D), jnp.float32),
        'ffn2_norm': gain(ks[26], (L, D)),
        'ffn2_w_in': D ** -0.5 * nrm(ks[27], (L, D, 2 * F), jnp.float32),
        'ffn2_w_out': F ** -0.5 * nrm(ks[28], (L, F, D), jnp.float32),
    }


def reference(x, c, positions, ada_w, ada_b, ffn1_norm, ffn1_w_in, ffn1_w_out, mix_norm, mix_w_in,
              m_gate_b, m_conv_w, m_conv_b, m_wq, m_wk, m_out_norm, m_skip, a_q_norm, a_k_norm,
              a_sinks, proj_a, proj_b, merge_w, merge_b, w_out, ffn2_norm, ffn2_w_in, ffn2_w_out):
    c_act = jax.nn.silu(c)
    for l in range(DEPTH):
        mod = (c_act @ ada_w[l] + ada_b[l])[:, None, :]
        sh1, sc1, g1, sh2, sc2, g2, sh3, sc3, g3 = jnp.split(mod, 9, axis=-1)
        h = modulate(x, ffn1_norm[l], sh1, sc1)
        x = x + 0.5 * g1 * swiglu(h, ffn1_w_in[l], ffn1_w_out[l])
        h = modulate(x, mix_norm[l], sh2, sc2)
        x = x + g2 * hybrid_mixer(h, positions, mix_w_in[l], m_gate_b[l], m_conv_w[l], m_conv_b[l],
                                  m_wq[l], m_wk[l], m_out_norm[l], m_skip[l], a_q_norm[l],
                                  a_k_norm[l], a_sinks[l], proj_a[l], proj_b[l], merge_w[l],
                                  merge_b[l], w_out[l])
        h = modulate(x, ffn2_norm[l], sh3, sc3)
        x = x + 0.5 * g3 * swiglu(h, ffn2_w_in[l], ffn2_w_out[l])
    return x
```

```python
import functools

import jax
import jax.numpy as jnp
from jax import lax
from jax.experimental import pallas as pl
from jax.experimental.pallas import tpu as pltpu

D_MODEL = 1024
DEPTH = 2
CHUNK = 64
M_HEADS = 4
M_HEAD_DIM = 256
M_WIDTH = M_HEADS * M_HEAD_DIM
M_CONV = 4
A_HEADS = 16
A_KV_HEADS = 4
A_GROUP = A_HEADS // A_KV_HEADS
A_HEAD_DIM = 64
A_Q_WIDTH = A_HEADS * A_HEAD_DIM
A_KV_WIDTH = A_KV_HEADS * A_HEAD_DIM
A_PREV_CHUNKS = 2
ROPE_THETA = 10000.0
D_FF = 2816
EPS = 1e-6

LANES = 128
SUBLANES = 8
MXU_DIM = 256

FFN_TILE = 512
FFN_CHUNK = 256
MIX_TILE = 256
ADA_BLOCK = 1536
VMEM_LIMIT = 56 * 1024 * 1024

_NT = (((1,), (1,)), ((), ()))
_TN = (((0,), (0,)), ((), ()))


def _bf(x):
    return x.astype(jnp.bfloat16)


def _dot(a, b):
    return jnp.dot(a, b, preferred_element_type=jnp.float32)


def _modulate(x, norm_w, shift, scale):
    var = jnp.mean(x * x, axis=-1, keepdims=True)
    y = x * lax.rsqrt(var + EPS)
    return (y * norm_w) * (1.0 + scale) + shift


def _sigmoid(x):
    return 1.0 / (1.0 + jnp.exp(-x))


def _log_sigmoid(x):
    return jnp.minimum(x, 0.0) - jnp.log(1.0 + jnp.exp(-jnp.abs(x)))


def _split3(x):
    hi = _bf(x)
    r = x - hi.astype(jnp.float32)
    mid = _bf(r)
    lo = _bf(r - mid.astype(jnp.float32))
    return hi, mid, lo


def _const_spec(shape):
    nd = len(shape)
    return pl.BlockSpec(shape, lambda *_: (0,) * nd, pipeline_mode=pl.Buffered(1))


def _ada_kernel(c_ref, w_ref, b_ref, o_ref):
    c = c_ref[...]
    c_act = c * _sigmoid(c)
    o_ref[0] = jnp.dot(c_act, w_ref[0], preferred_element_type=jnp.float32,
                       precision=lax.Precision.HIGHEST) + b_ref[0]


def _ada_call(c, ada_w, ada_b):
    batch = c.shape[0]
    n_out = ada_w.shape[-1]
    return pl.pallas_call(
        _ada_kernel,
        grid=(DEPTH, n_out // ADA_BLOCK),
        in_specs=[
            pl.BlockSpec((batch, D_MODEL), lambda l, j: (0, 0)),
            pl.BlockSpec((1, D_MODEL, ADA_BLOCK), lambda l, j: (l, 0, j)),
            pl.BlockSpec((1, 1, ADA_BLOCK), lambda l, j: (l, 0, j)),
        ],
        out_specs=pl.BlockSpec((1, batch, ADA_BLOCK), lambda l, j: (l, 0, j)),
        out_shape=jax.ShapeDtypeStruct((DEPTH, batch, n_out), jnp.float32),
        compiler_params=pltpu.CompilerParams(
            dimension_semantics=("arbitrary", "arbitrary"), vmem_limit_bytes=VMEM_LIMIT),
        name="adaln_mod",
    )(c, ada_w, ada_b.reshape(DEPTH, 1, n_out))


def _rope_kernel(pos_ref, freq_ref, o_ref):
    ang = pos_ref[0].astype(jnp.float32) * freq_ref[...]
    lane = lax.broadcasted_iota(jnp.int32, ang.shape, 1)
    first_half = (lane % A_HEAD_DIM) < (A_HEAD_DIM // 2)
    o_ref[0] = jnp.where(first_half, jnp.cos(ang), jnp.sin(ang))


def _rope_call(positions):
    batch, seq = positions.shape
    half = A_HEAD_DIM // 2
    inv_freq = ROPE_THETA ** (-jnp.arange(0, A_HEAD_DIM, 2, dtype=jnp.float32) / A_HEAD_DIM)
    freq_row = jnp.tile(inv_freq, LANES // half).reshape(1, LANES)
    rows = 512
    return pl.pallas_call(
        _rope_kernel,
        grid=(batch, seq // rows),
        in_specs=[
            pl.BlockSpec((1, rows, 1), lambda b, j: (b, j, 0)),
            pl.BlockSpec((1, LANES), lambda b, j: (0, 0)),
        ],
        out_specs=pl.BlockSpec((1, rows, LANES), lambda b, j: (b, j, 0)),
        out_shape=jax.ShapeDtypeStruct((batch, seq, LANES), jnp.float32),
        compiler_params=pltpu.CompilerParams(
            dimension_semantics=("arbitrary", "arbitrary"), vmem_limit_bytes=VMEM_LIMIT),
        name="rope_table",
    )(positions.reshape(batch, seq, 1), freq_row)


def _ffn_kernel(row0, x_ref, mod_ref, nw_ref, win_ref, wout_ref, o_ref, act_ref):
    x = x_ref[0]
    mod = mod_ref[0]
    shift, scale, gate = mod[row0:row0 + 1], mod[row0 + 1:row0 + 2], mod[row0 + 2:row0 + 3]
    hb = _bf(_modulate(x, nw_ref[...], shift, scale))
    for c in range(D_FF // FFN_CHUNK):
        lo = c * FFN_CHUNK
        a = _dot(hb, win_ref[:, lo:lo + FFN_CHUNK])
        g = _dot(hb, win_ref[:, D_FF + lo:D_FF + lo + FFN_CHUNK])
        act_ref[:, lo:lo + FFN_CHUNK] = _bf(a * _sigmoid(a) * g)
    y = _dot(act_ref[...], wout_ref[...])
    o_ref[0] = x + (0.5 * gate) * y


def _ffn_call(x, mod, row0, norm_w, w_in, w_out):
    batch, seq, _ = x.shape
    tile = FFN_TILE
    return pl.pallas_call(
        functools.partial(_ffn_kernel, row0),
        grid=(batch, seq // tile),
        in_specs=[
            pl.BlockSpec((1, tile, D_MODEL), lambda b, j: (b, j, 0)),
            pl.BlockSpec((1, 9, D_MODEL), lambda b, j: (b, 0, 0)),
            _const_spec((1, D_MODEL)),
            _const_spec((D_MODEL, 2 * D_FF)),
            _const_spec((D_FF, D_MODEL)),
        ],
        out_specs=pl.BlockSpec((1, tile, D_MODEL), lambda b, j: (b, j, 0)),
        out_shape=jax.ShapeDtypeStruct(x.shape, x.dtype),
        scratch_shapes=[pltpu.VMEM((tile, D_FF), jnp.bfloat16)],
        compiler_params=pltpu.CompilerParams(
            dimension_semantics=("arbitrary", "arbitrary"), vmem_limit_bytes=VMEM_LIMIT),
        name="ffn",
    )(x, mod, norm_w.reshape(1, D_MODEL), w_in, w_out)


def _mixer_kernel(x_ref, mod_ref, tab_ref, nw_ref, wmain_ref, wgate_ref, wgate_t_ref,
                  gb_col_ref, gb_row_ref, convw_ref, convb_ref, wq_ref, wk_ref,
                  onorm_ref, skip_ref, qn_ref, kn_ref, sink_ref, bd_ref,
                  proja_ref, projb_ref, mergew_ref, mergeb_ref, wout_ref,
                  o_ref,
                  uext_ref, c_ref, n_ref, m_ref, kext_ref, vext_ref, ya_ref, yb_ref):
    j = pl.program_id(1)
    ts = MIX_TILE
    f32 = jnp.float32
    carry_rows = A_PREV_CHUNKS * CHUNK

    @pl.when(j == 0)
    def _reset():
        uext_ref[0:SUBLANES, :] = jnp.zeros((SUBLANES, M_WIDTH), f32)
        c_ref[...] = jnp.zeros(c_ref.shape, f32)
        n_ref[...] = jnp.zeros(n_ref.shape, f32)
        m_ref[...] = jnp.zeros(m_ref.shape, f32)
        kext_ref[0:carry_rows, :] = jnp.zeros((carry_rows, kext_ref.shape[1]), jnp.bfloat16)
        vext_ref[0:carry_rows, :] = jnp.zeros((carry_rows, vext_ref.shape[1]), jnp.bfloat16)

    x = x_ref[0]
    mod = mod_ref[0]
    hb = _bf(_modulate(x, nw_ref[...], mod[3:4], mod[4:5]))

    uext_ref[SUBLANES:SUBLANES + ts, :] = _dot(hb, wmain_ref[:, 0:M_WIDTH])
    uc = convb_ref[...] + convw_ref[M_CONV - 1:M_CONV, :] * uext_ref[SUBLANES:SUBLANES + ts, :]
    for k in range(1, M_CONV):
        uc = uc + convw_ref[M_CONV - 1 - k:M_CONV - k, :] * uext_ref[SUBLANES - k:SUBLANES - k + ts, :]
    uext_ref[0:SUBLANES, :] = uext_ref[ts:ts + SUBLANES, :]
    ua = uc * _sigmoid(uc)
    uab = _bf(ua)

    gcol = _dot(hb, wgate_ref[...]) + gb_col_ref[...]
    grow = lax.dot_general(wgate_t_ref[...], hb, _NT,
                           preferred_element_type=f32) + gb_row_ref[...][:, 0:1]
    rr = lax.broadcasted_iota(jnp.int32, (ts, ts), 0)
    cc = lax.broadcasted_iota(jnp.int32, (ts, ts), 1)
    causal = rr >= cc
    tri = jnp.where(causal, 1.0, 0.0).astype(jnp.bfloat16)
    tri_t = jnp.where(rr <= cc, 1.0, 0.0).astype(jnp.bfloat16)
    b_col = sum(_dot(tri, p) for p in _split3(_log_sigmoid(gcol)))
    b_row = sum(_dot(p, tri_t) for p in _split3(_log_sigmoid(grow)))

    for h in range(M_HEADS):
        lo = h * M_HEAD_DIM
        hi = lo + M_HEAD_DIM
        ua_h = uab[:, lo:hi]
        q = _dot(ua_h, wq_ref[h])
        k = _dot(ua_h, wk_ref[h]) * (M_HEAD_DIM ** -0.5)
        qb, kb = _bf(q), _bf(k)
        vb = _bf(_dot(hb, wmain_ref[:, M_WIDTH + lo:M_WIDTH + hi]))
        bt = b_col[:, M_HEADS + h:M_HEADS + h + 1]
        bs = b_row[M_HEADS + h:M_HEADS + h + 1, :]
        li_t = gcol[:, h:h + 1]
        li_s = grow[h:h + 1, :]
        m_prev = m_ref[h][:, 0:1]
        dmat = jnp.where(causal, bt - bs + li_s, -jnp.inf)
        a = bt + m_prev
        m_t = jnp.maximum(a, jnp.max(dmat, axis=-1, keepdims=True))
        w_intra = jnp.exp(dmat - m_t)
        w_inter = jnp.exp(a - m_t)
        s = lax.dot_general(qb, kb, _NT, preferred_element_type=f32) * w_intra
        c_state = c_ref[h]
        n_state = n_ref[h]
        num = _dot(_bf(s), vb) + w_inter * _dot(qb, _bf(c_state))
        den = jnp.sum(s, axis=-1, keepdims=True) + w_inter * jnp.sum(q * n_state, axis=-1, keepdims=True)
        hh = num / jnp.maximum(jnp.abs(den), jnp.exp(-m_t))
        b_last = bt[ts - 1:ts, :]
        g = b_last - bt + li_t
        a_end = b_last + m_prev
        m_new = jnp.maximum(a_end, jnp.max(g, axis=0, keepdims=True))
        wg = jnp.exp(g - m_new)
        decay = jnp.exp(a_end - m_new)
        kw = k * wg
        c_ref[h] = decay * c_state + lax.dot_general(_bf(kw), vb, _TN, preferred_element_type=f32)
        n_ref[h] = decay * n_state + jnp.sum(kw, axis=0, keepdims=True)
        m_ref[h] = jnp.broadcast_to(m_new, (1, LANES))
        hn = hh * lax.rsqrt(jnp.mean(hh * hh, axis=-1, keepdims=True) + EPS)
        o_pre = _dot(hb, wmain_ref[:, 2 * M_WIDTH + lo:2 * M_WIDTH + hi])
        ya = _sigmoid(o_pre) * (hn * onorm_ref[:, lo:hi] + skip_ref[:, lo:hi] * ua[:, lo:hi])
        ya_ref[:, lo:hi] = _bf(ya)

    tab = tab_ref[0]
    lane = lax.broadcasted_iota(jnp.int32, (ts, LANES), 1)
    half = A_HEAD_DIM // 2
    first_half = (lane % A_HEAD_DIM) < half
    low_head = lane < A_HEAD_DIM
    cos_t = jnp.where(first_half, tab, pltpu.roll(tab, half, 1))
    sin_lo = jnp.where(first_half, -pltpu.roll(tab, LANES - half, 1), 0.0)
    sin_hi = jnp.where(first_half, 0.0, tab)
    bd = bd_ref[...]

    def norm_rope(t, nw):
        ss = _dot(_bf(t * t), bd)
        t = t * lax.rsqrt(ss * (1.0 / A_HEAD_DIM) + EPS)
        out = []
        for i in range(MXU_DIM // LANES):
            tb = t[:, i * LANES:(i + 1) * LANES] * nw
            out.append(tb * cos_t + pltpu.roll(tb, LANES - half, 1) * sin_lo
                       + pltpu.roll(tb, half, 1) * sin_hi)
        return out

    q_off = 3 * M_WIDTH
    k_off = q_off + A_Q_WIDTH
    v_off = k_off + A_KV_WIDTH
    kk = norm_rope(_dot(hb, wmain_ref[:, k_off:k_off + A_KV_WIDTH]), kn_ref[...])
    vv = _dot(hb, wmain_ref[:, v_off:v_off + A_KV_WIDTH])
    for g in range(A_KV_HEADS):
        kcol = kk[g // 2]
        vcol = vv[:, (g // 2) * LANES:(g // 2 + 1) * LANES]
        kr = pltpu.roll(kcol, A_HEAD_DIM, 1)
        vr = pltpu.roll(vcol, A_HEAD_DIM, 1)
        if g % 2 == 0:
            kd, vd = jnp.where(low_head, kcol, kr), jnp.where(low_head, vcol, vr)
        else:
            kd, vd = jnp.where(low_head, kr, kcol), jnp.where(low_head, vr, vcol)
        kext_ref[carry_rows:carry_rows + ts, g * LANES:(g + 1) * LANES] = _bf(kd)
        vext_ref[carry_rows:carry_rows + ts, g * LANES:(g + 1) * LANES] = _bf(vd)

    keys = (A_PREV_CHUNKS + 1) * CHUNK
    key_idx = lax.broadcasted_iota(jnp.int32, (1, keys), 1)
    mask_a = lax.broadcasted_iota(jnp.int32, (CHUNK, LANES), 1) < A_HEAD_DIM
    for g in range(A_KV_HEADS):
        qq = norm_rope(_dot(hb, wmain_ref[:, q_off + g * MXU_DIM:q_off + (g + 1) * MXU_DIM]), qn_ref[...])
        qq = [t * (A_HEAD_DIM ** -0.5) for t in qq]
        sink = sink_ref[g][:, 0:1]
        for c in range(ts // CHUNK):
            r0 = c * CHUNK
            parts = []
            for t in qq:
                tc = t[r0:r0 + CHUNK]
                parts += [jnp.where(mask_a, tc, 0.0), jnp.where(mask_a, 0.0, tc)]
            qs = _bf(jnp.concatenate(parts, axis=0))
            kd = kext_ref[r0:r0 + keys, g * LANES:(g + 1) * LANES]
            vd = vext_ref[r0:r0 + keys, g * LANES:(g + 1) * LANES]
            s = lax.dot_general(qs, kd, _NT, preferred_element_type=f32)
            n_invalid = carry_rows - r0
            if n_invalid > 0:
                s = s + jnp.where((key_idx < n_invalid) & (j == 0), -jnp.inf, 0.0)
            m = jnp.maximum(jnp.max(s, axis=-1, keepdims=True), sink)
            p = jnp.exp(s - m)
            denom = jnp.sum(p, axis=-1, keepdims=True) + jnp.exp(sink - m)
            o = _dot(_bf(p), vd) / denom
            for i in range(2):
                pair = jnp.where(mask_a, o[(2 * i) * CHUNK:(2 * i + 1) * CHUNK],
                                 o[(2 * i + 1) * CHUNK:(2 * i + 2) * CHUNK])
                col = g * MXU_DIM + i * LANES
                yb_ref[r0:r0 + CHUNK, col:col + LANES] = _bf(pair)
    kext_ref[0:carry_rows, :] = kext_ref[ts:ts + carry_rows, :]
    vext_ref[0:carry_rows, :] = vext_ref[ts:ts + carry_rows, :]

    gates = _sigmoid(_dot(hb, mergew_ref[...]) + mergeb_ref[...])
    merged = (gates[:, 0:D_MODEL] * _dot(ya_ref[...], proja_ref[...])
              + gates[:, D_MODEL:2 * D_MODEL] * _dot(yb_ref[...], projb_ref[...]))
    y = _dot(_bf(merged), wout_ref[...])
    o_ref[0] = x + mod[5:6] * y


def _mixer_call(x, mod, tab, p):
    batch, seq, _ = x.shape
    ts = MIX_TILE
    carry_rows = A_PREV_CHUNKS * CHUNK
    consts = [p["norm"], p["w_main"], p["w_gate"], p["w_gate_t"], p["gb_col"], p["gb_row"],
              p["conv_w"], p["conv_b"], p["wq"], p["wk"], p["out_norm"], p["skip"],
              p["q_norm"], p["k_norm"], p["sink"], p["bd"], p["proj_a"], p["proj_b"],
              p["merge_w"], p["merge_b"], p["w_out"]]
    return pl.pallas_call(
        _mixer_kernel,
        grid=(batch, seq // ts),
        in_specs=[
            pl.BlockSpec((1, ts, D_MODEL), lambda b, j: (b, j, 0)),
            pl.BlockSpec((1, 9, D_MODEL), lambda b, j: (b, 0, 0)),
            pl.BlockSpec((1, ts, LANES), lambda b, j: (b, j, 0)),
        ] + [_const_spec(a.shape) for a in consts],
        out_specs=pl.BlockSpec((1, ts, D_MODEL), lambda b, j: (b, j, 0)),
        out_shape=jax.ShapeDtypeStruct(x.shape, x.dtype),
        scratch_shapes=[
            pltpu.VMEM((SUBLANES + ts, M_WIDTH), jnp.float32),
            pltpu.VMEM((M_HEADS, M_HEAD_DIM, M_HEAD_DIM), jnp.float32),
            pltpu.VMEM((M_HEADS, 1, M_HEAD_DIM), jnp.float32),
            pltpu.VMEM((M_HEADS, 1, LANES), jnp.float32),
            pltpu.VMEM((carry_rows + ts, A_KV_HEADS * LANES), jnp.bfloat16),
            pltpu.VMEM((carry_rows + ts, A_KV_HEADS * LANES), jnp.bfloat16),
            pltpu.VMEM((ts, M_WIDTH), jnp.bfloat16),
            pltpu.VMEM((ts, A_Q_WIDTH), jnp.bfloat16),
        ],
        compiler_params=pltpu.CompilerParams(
            dimension_semantics=("arbitrary", "arbitrary"), vmem_limit_bytes=VMEM_LIMIT),
        name="mixer",
    )(x, mod, tab, *consts)


def _mixer_params(l, mix_norm, mix_w_in, m_gate_b, m_conv_w, m_conv_b, m_wq, m_wk, m_out_norm,
                  m_skip, a_q_norm, a_k_norm, a_sinks, proj_a, proj_b, merge_w, merge_b, w_out):
    w = mix_w_in[l]
    g0 = 3 * M_WIDTH
    g1 = g0 + 2 * M_HEADS
    w_gate = w[:, g0:g1]
    row = lambda v: v.reshape(1, -1)
    sink = jnp.repeat(a_sinks[l].reshape(A_KV_HEADS, A_GROUP), CHUNK, axis=1)
    head = jnp.arange(MXU_DIM) // A_HEAD_DIM
    return {
        "norm": row(mix_norm[l]),
        "w_main": _bf(jnp.concatenate([w[:, :g0], w[:, g1:]], axis=1)),
        "w_gate": _bf(jnp.pad(w_gate, ((0, 0), (0, LANES - 2 * M_HEADS)))),
        "w_gate_t": _bf(jnp.pad(w_gate.T, ((0, 2 * SUBLANES - 2 * M_HEADS), (0, 0)))),
        "gb_col": jnp.pad(m_gate_b[l], (0, LANES - 2 * M_HEADS)).reshape(1, LANES),
        "gb_row": jnp.broadcast_to(
            jnp.pad(m_gate_b[l], (0, 2 * SUBLANES - 2 * M_HEADS))[:, None], (2 * SUBLANES, LANES)),
        "conv_w": m_conv_w[l],
        "conv_b": row(m_conv_b[l]),
        "wq": _bf(m_wq[l]),
        "wk": _bf(m_wk[l]),
        "out_norm": row(m_out_norm[l]),
        "skip": row(m_skip[l]),
        "q_norm": row(jnp.tile(a_q_norm[l], LANES // A_HEAD_DIM)),
        "k_norm": row(jnp.tile(a_k_norm[l], LANES // A_HEAD_DIM)),
        "sink": jnp.broadcast_to(sink[:, :, None], (A_KV_HEADS, A_GROUP * CHUNK, LANES)),
        "bd": _bf(head[:, None] == head[None, :]),
        "proj_a": _bf(proj_a[l]),
        "proj_b": _bf(proj_b[l]),
        "merge_w": _bf(merge_w[l]),
        "merge_b": row(merge_b[l]),
        "w_out": _bf(w_out[l]),
    }


def kernel(x, c, positions, ada_w, ada_b, ffn1_norm, ffn1_w_in, ffn1_w_out, mix_norm, mix_w_in, m_gate_b, m_conv_w, m_conv_b, m_wq, m_wk, m_out_norm, m_skip, a_q_norm, a_k_norm, a_sinks, proj_a, proj_b, merge_w, merge_b, w_out, ffn2_norm, ffn2_w_in, ffn2_w_out):
    batch = x.shape[0]
    mod_all = _ada_call(c, ada_w, ada_b).reshape(DEPTH, batch, 9, D_MODEL)
    tab = _rope_call(positions)
    for l in range(DEPTH):
        mod = mod_all[l]
        x = _ffn_call(x, mod, 0, ffn1_norm[l], _bf(ffn1_w_in[l]), _bf(ffn1_w_out[l]))
        x = _mixer_call(x, mod, tab, _mixer_params(
            l, mix_norm, mix_w_in, m_gate_b, m_conv_w, m_conv_b, m_wq, m_wk, m_out_norm, m_skip,
            a_q_norm, a_k_norm, a_sinks, proj_a, proj_b, merge_w, merge_b, w_out))
        x = _ffn_call(x, mod, 6, ffn2_norm[l], _bf(ffn2_w_in[l]), _bf(ffn2_w_out[l]))
    return x
```

```python
import functools

import jax
import jax.numpy as jnp
from jax import lax
from jax.experimental import pallas as pl
from jax.experimental.pallas import tpu as pltpu

D_MODEL = 1024
DEPTH = 2
CHUNK = 64
M_HEADS = 4
M_HEAD_DIM = 256
M_WIDTH = M_HEADS * M_HEAD_DIM
M_CONV = 4
A_HEADS = 16
A_KV_HEADS = 4
A_GROUP = A_HEADS // A_KV_HEADS
A_HEAD_DIM = 64
A_Q_WIDTH = A_HEADS * A_HEAD_DIM
A_KV_WIDTH = A_KV_HEADS * A_HEAD_DIM
A_PREV_CHUNKS = 2
ROPE_THETA = 10000.0
D_FF = 2816
EPS = 1e-6

LANES = 128
SUBLANES = 8
MXU_DIM = 256

FFN_TILE = 512
FFN_CHUNK = 256
MIX_TILE = 256
ADA_BLOCK = 1536
VMEM_LIMIT = 56 * 1024 * 1024

_NT = (((1,), (1,)), ((), ()))
_TN = (((0,), (0,)), ((), ()))


def _bf(x):
    return x.astype(jnp.bfloat16)


def _dot(a, b):
    return jnp.dot(a, b, preferred_element_type=jnp.float32)


def _modulate(x, norm_w, shift, scale):
    var = jnp.mean(x * x, axis=-1, keepdims=True)
    y = x * lax.rsqrt(var + EPS)
    return (y * norm_w) * (1.0 + scale) + shift


def _sigmoid(x):
    return 1.0 / (1.0 + jnp.exp(-x))


def _log_sigmoid(x):
    return jnp.minimum(x, 0.0) - jnp.log(1.0 + jnp.exp(-jnp.abs(x)))


def _split3(x):
    hi = _bf(x)
    r = x - hi.astype(jnp.float32)
    mid = _bf(r)
    lo = _bf(r - mid.astype(jnp.float32))
    return hi, mid, lo


def _const_spec(shape):
    nd = len(shape)
    return pl.BlockSpec(shape, lambda *_: (0,) * nd, pipeline_mode=pl.Buffered(1))


def _ada_kernel(c_ref, w_ref, b_ref, o_ref):
    c = c_ref[...]
    c_act = c * _sigmoid(c)
    o_ref[0] = jnp.dot(c_act, w_ref[0], preferred_element_type=jnp.float32,
                       precision=lax.Precision.HIGHEST) + b_ref[0]


def _ada_call(c, ada_w, ada_b):
    batch = c.shape[0]
    n_out = ada_w.shape[-1]
    return pl.pallas_call(
        _ada_kernel,
        grid=(DEPTH, n_out // ADA_BLOCK),
        in_specs=[
            pl.BlockSpec((batch, D_MODEL), lambda l, j: (0, 0)),
            pl.BlockSpec((1, D_MODEL, ADA_BLOCK), lambda l, j: (l, 0, j)),
            pl.BlockSpec((1, 1, ADA_BLOCK), lambda l, j: (l, 0, j)),
        ],
        out_specs=pl.BlockSpec((1, batch, ADA_BLOCK), lambda l, j: (l, 0, j)),
        out_shape=jax.ShapeDtypeStruct((DEPTH, batch, n_out), jnp.float32),
        compiler_params=pltpu.CompilerParams(
            dimension_semantics=("arbitrary", "arbitrary"), vmem_limit_bytes=VMEM_LIMIT),
        name="adaln_mod",
    )(c, ada_w, ada_b.reshape(DEPTH, 1, n_out))


def _rope_kernel(pos_ref, freq_ref, o_ref):
    ang = pos_ref[0].astype(jnp.float32) * freq_ref[...]
    lane = lax.broadcasted_iota(jnp.int32, ang.shape, 1)
    first_half = (lane % A_HEAD_DIM) < (A_HEAD_DIM // 2)
    o_ref[0] = jnp.where(first_half, jnp.cos(ang), jnp.sin(ang))


def _rope_call(positions):
    batch, seq = positions.shape
    half = A_HEAD_DIM // 2
    inv_freq = ROPE_THETA ** (-jnp.arange(0, A_HEAD_DIM, 2, dtype=jnp.float32) / A_HEAD_DIM)
    freq_row = jnp.tile(inv_freq, LANES // half).reshape(1, LANES)
    rows = 512
    return pl.pallas_call(
        _rope_kernel,
        grid=(batch, seq // rows),
        in_specs=[
            pl.BlockSpec((1, rows, 1), lambda b, j: (b, j, 0)),
            pl.BlockSpec((1, LANES), lambda b, j: (0, 0)),
        ],
        out_specs=pl.BlockSpec((1, rows, LANES), lambda b, j: (b, j, 0)),
        out_shape=jax.ShapeDtypeStruct((batch, seq, LANES), jnp.float32),
        compiler_params=pltpu.CompilerParams(
            dimension_semantics=("arbitrary", "arbitrary"), vmem_limit_bytes=VMEM_LIMIT),
        name="rope_table",
    )(positions.reshape(batch, seq, 1), freq_row)


def _ffn_kernel(row0, x_ref, mod_ref, nw_ref, win_ref, wout_ref, o_ref, act_ref):
    x = x_ref[0]
    mod = mod_ref[0]
    shift, scale, gate = mod[row0:row0 + 1], mod[row0 + 1:row0 + 2], mod[row0 + 2:row0 + 3]
    hb = _bf(_modulate(x, nw_ref[...], shift, scale))
    for c in range(D_FF // FFN_CHUNK):
        lo = c * FFN_CHUNK
        a = _dot(hb, win_ref[:, lo:lo + FFN_CHUNK])
        g = _dot(hb, win_ref[:, D_FF + lo:D_FF + lo + FFN_CHUNK])
        act_ref[:, lo:lo + FFN_CHUNK] = _bf(a * _sigmoid(a) * g)
    y = _dot(act_ref[...], wout_ref[...])
    o_ref[0] = x + (0.5 * gate) * y


def _ffn_call(x, mod, row0, norm_w, w_in, w_out):
    batch, seq, _ = x.shape
    tile = FFN_TILE
    return pl.pallas_call(
        functools.partial(_ffn_kernel, row0),
        grid=(batch, seq // tile),
        in_specs=[
            pl.BlockSpec((1, tile, D_MODEL), lambda b, j: (b, j, 0)),
            pl.BlockSpec((1, 9, D_MODEL), lambda b, j: (b, 0, 0)),
            _const_spec((1, D_MODEL)),
            _const_spec((D_MODEL, 2 * D_FF)),
            _const_spec((D_FF, D_MODEL)),
        ],
        out_specs=pl.BlockSpec((1, tile, D_MODEL), lambda b, j: (b, j, 0)),
        out_shape=jax.ShapeDtypeStruct(x.shape, x.dtype),
        scratch_shapes=[pltpu.VMEM((tile, D_FF), jnp.bfloat16)],
        compiler_params=pltpu.CompilerParams(
            dimension_semantics=("arbitrary", "arbitrary"), vmem_limit_bytes=VMEM_LIMIT),
        name="ffn",
    )(x, mod, norm_w.reshape(1, D_MODEL), w_in, w_out)


def _mixer_kernel(x_ref, mod_ref, tab_ref, nw_ref, wmain_ref, wvt_ref, wgate_ref, wgate_t_ref,
                  gb_col_ref, gb_row_ref, convw_ref, convb_ref, wq_ref, wk_ref,
                  onorm_ref, skip_ref, qn_ref, kn_ref, sink_ref, bd_ref,
                  proja_ref, projb_ref, mergew_ref, mergeb_ref, wout_ref,
                  o_ref,
                  uext_ref, c_ref, n_ref, m_ref, kext_ref, vext_ref, ya_ref, ybt_ref):
    j = pl.program_id(1)
    ts = MIX_TILE
    f32 = jnp.float32
    carry_rows = A_PREV_CHUNKS * CHUNK

    @pl.when(j == 0)
    def _reset():
        uext_ref[0:SUBLANES, :] = jnp.zeros((SUBLANES, M_WIDTH), f32)
        c_ref[...] = jnp.zeros(c_ref.shape, f32)
        n_ref[...] = jnp.zeros(n_ref.shape, f32)
        m_ref[...] = jnp.zeros(m_ref.shape, f32)
        kext_ref[0:carry_rows, :] = jnp.zeros((carry_rows, kext_ref.shape[1]), jnp.bfloat16)
        vext_ref[:, 0:carry_rows] = jnp.zeros((vext_ref.shape[0], carry_rows), jnp.bfloat16)

    x = x_ref[0]
    mod = mod_ref[0]
    hb = _bf(_modulate(x, nw_ref[...], mod[3:4], mod[4:5]))

    uext_ref[SUBLANES:SUBLANES + ts, :] = _dot(hb, wmain_ref[:, 0:M_WIDTH])
    uc = convb_ref[...] + convw_ref[M_CONV - 1:M_CONV, :] * uext_ref[SUBLANES:SUBLANES + ts, :]
    for k in range(1, M_CONV):
        uc = uc + convw_ref[M_CONV - 1 - k:M_CONV - k, :] * uext_ref[SUBLANES - k:SUBLANES - k + ts, :]
    uext_ref[0:SUBLANES, :] = uext_ref[ts:ts + SUBLANES, :]
    ua = uc * _sigmoid(uc)
    uab = _bf(ua)

    gcol = _dot(hb, wgate_ref[...]) + gb_col_ref[...]
    grow = lax.dot_general(wgate_t_ref[...], hb, _NT,
                           preferred_element_type=f32) + gb_row_ref[...][:, 0:1]
    rr = lax.broadcasted_iota(jnp.int32, (ts, ts), 0)
    cc = lax.broadcasted_iota(jnp.int32, (ts, ts), 1)
    causal = rr >= cc
    tri = jnp.where(causal, 1.0, 0.0).astype(jnp.bfloat16)
    tri_t = jnp.where(rr <= cc, 1.0, 0.0).astype(jnp.bfloat16)
    b_col = sum(_dot(tri, p) for p in _split3(_log_sigmoid(gcol)))
    b_row = sum(_dot(p, tri_t) for p in _split3(_log_sigmoid(grow)))

    for h in range(M_HEADS):
        lo = h * M_HEAD_DIM
        hi = lo + M_HEAD_DIM
        ua_h = uab[:, lo:hi]
        q = _dot(ua_h, wq_ref[h])
        k = _dot(ua_h, wk_ref[h]) * (M_HEAD_DIM ** -0.5)
        qb, kb = _bf(q), _bf(k)
        vb = _bf(_dot(hb, wmain_ref[:, M_WIDTH + lo:M_WIDTH + hi]))
        bt = b_col[:, M_HEADS + h:M_HEADS + h + 1]
        bs = b_row[M_HEADS + h:M_HEADS + h + 1, :]
        li_t = gcol[:, h:h + 1]
        li_s = grow[h:h + 1, :]
        m_prev = m_ref[h][:, 0:1]
        dmat = jnp.where(causal, bt - bs + li_s, -jnp.inf)
        a = bt + m_prev
        m_t = jnp.maximum(a, jnp.max(dmat, axis=-1, keepdims=True))
        w_intra = jnp.exp(dmat - m_t)
        w_inter = jnp.exp(a - m_t)
        s = lax.dot_general(qb, kb, _NT, preferred_element_type=f32) * w_intra
        c_state = c_ref[h]
        n_state = n_ref[h]
        num = _dot(_bf(s), vb) + w_inter * _dot(qb, _bf(c_state))
        den = jnp.sum(s, axis=-1, keepdims=True) + w_inter * jnp.sum(q * n_state, axis=-1, keepdims=True)
        hh = num / jnp.maximum(jnp.abs(den), jnp.exp(-m_t))
        b_last = bt[ts - 1:ts, :]
        g = b_last - bt + li_t
        a_end = b_last + m_prev
        m_new = jnp.maximum(a_end, jnp.max(g, axis=0, keepdims=True))
        wg = jnp.exp(g - m_new)
        decay = jnp.exp(a_end - m_new)
        kw = k * wg
        c_ref[h] = decay * c_state + lax.dot_general(_bf(kw), vb, _TN, preferred_element_type=f32)
        n_ref[h] = decay * n_state + jnp.sum(kw, axis=0, keepdims=True)
        m_ref[h] = jnp.broadcast_to(m_new, (1, LANES))
        hn = hh * lax.rsqrt(jnp.mean(hh * hh, axis=-1, keepdims=True) + EPS)
        o_pre = _dot(hb, wmain_ref[:, 2 * M_WIDTH + lo:2 * M_WIDTH + hi])
        ya = _sigmoid(o_pre) * (hn * onorm_ref[:, lo:hi] + skip_ref[:, lo:hi] * ua[:, lo:hi])
        ya_ref[:, lo:hi] = _bf(ya)

    tab = tab_ref[0]
    lane = lax.broadcasted_iota(jnp.int32, (ts, LANES), 1)
    half = A_HEAD_DIM // 2
    first_half = (lane % A_HEAD_DIM) < half
    low_head = lane < A_HEAD_DIM
    cos_t = jnp.where(first_half, tab, pltpu.roll(tab, half, 1))
    sin_lo = jnp.where(first_half, -pltpu.roll(tab, LANES - half, 1), 0.0)
    sin_hi = jnp.where(first_half, 0.0, tab)
    bd = bd_ref[...]

    def norm_rope(t, nw):
        ss = _dot(_bf(t * t), bd)
        t = t * lax.rsqrt(ss * (1.0 / A_HEAD_DIM) + EPS)
        out = []
        for i in range(MXU_DIM // LANES):
            tb = t[:, i * LANES:(i + 1) * LANES] * nw
            out.append(tb * cos_t + pltpu.roll(tb, LANES - half, 1) * sin_lo
                       + pltpu.roll(tb, half, 1) * sin_hi)
        return out

    q_off = 3 * M_WIDTH
    k_off = q_off + A_Q_WIDTH
    kk = norm_rope(_dot(hb, wmain_ref[:, k_off:k_off + A_KV_WIDTH]), kn_ref[...])
    for g in range(A_KV_HEADS):
        kcol = kk[g // 2]
        kr = pltpu.roll(kcol, A_HEAD_DIM, 1)
        kd = jnp.where(low_head, kcol, kr) if g % 2 == 0 else jnp.where(low_head, kr, kcol)
        kext_ref[carry_rows:carry_rows + ts, g * LANES:(g + 1) * LANES] = _bf(kd)
    vext_ref[:, carry_rows:carry_rows + ts] = _bf(
        lax.dot_general(wvt_ref[...], hb, _NT, preferred_element_type=f32))

    pair = 2 * CHUNK
    win = pair + carry_rows
    n_q = A_GROUP * pair
    key_row = lax.broadcasted_iota(jnp.int32, (win, n_q), 0)
    second_chunk = (lax.broadcasted_iota(jnp.int32, (win, n_q), 1) // CHUNK) % 2
    rel = key_row - second_chunk * CHUNK
    bias = jnp.where((rel >= 0) & (rel < win - CHUNK), 0.0, -jnp.inf)
    n_before_start = jnp.where(j == 0, carry_rows, 0)
    bias_first = jnp.where(key_row < n_before_start, -jnp.inf, bias)
    mask_a = lax.broadcasted_iota(jnp.int32, (pair, LANES), 1) < A_HEAD_DIM
    for g in range(A_KV_HEADS):
        qq = norm_rope(_dot(hb, wmain_ref[:, q_off + g * MXU_DIM:q_off + (g + 1) * MXU_DIM]), qn_ref[...])
        qq = [t * (A_HEAD_DIM ** -0.5) for t in qq]
        sink = sink_ref[g][0:1, :]
        for p in range(ts // pair):
            r0 = p * pair
            parts = []
            for t in qq:
                tc = t[r0:r0 + pair]
                parts += [jnp.where(mask_a, tc, 0.0), jnp.where(mask_a, 0.0, tc)]
            qs = _bf(jnp.concatenate(parts, axis=0))
            kd = kext_ref[r0:r0 + win, g * LANES:(g + 1) * LANES]
            vt = vext_ref[g * A_HEAD_DIM:(g + 1) * A_HEAD_DIM, r0:r0 + win]
            s = lax.dot_general(kd, qs, _NT, preferred_element_type=f32)
            s = s + (bias_first if p == 0 else bias)
            m = jnp.maximum(jnp.max(s, axis=0, keepdims=True), sink)
            pm = jnp.exp(s - m)
            denom = jnp.sum(pm, axis=0, keepdims=True) + jnp.exp(sink - m)
            o = _dot(vt, _bf(pm)) * (1.0 / denom)
            for h in range(A_GROUP):
                row = (g * A_GROUP + h) * A_HEAD_DIM
                ybt_ref[row:row + A_HEAD_DIM, r0:r0 + pair] = _bf(o[:, h * pair:(h + 1) * pair])
    kext_ref[0:carry_rows, :] = kext_ref[ts:ts + carry_rows, :]
    vext_ref[:, 0:carry_rows] = vext_ref[:, ts:ts + carry_rows]

    gates = _sigmoid(_dot(hb, mergew_ref[...]) + mergeb_ref[...])
    merged = (gates[:, 0:D_MODEL] * _dot(ya_ref[...], proja_ref[...])
              + gates[:, D_MODEL:2 * D_MODEL]
              * lax.dot_general(ybt_ref[...], projb_ref[...], _TN, preferred_element_type=f32))
    y = _dot(_bf(merged), wout_ref[...])
    o_ref[0] = x + mod[5:6] * y


def _mixer_call(x, mod, tab, p):
    batch, seq, _ = x.shape
    ts = MIX_TILE
    carry_rows = A_PREV_CHUNKS * CHUNK
    consts = [p["norm"], p["w_main"], p["w_v_t"], p["w_gate"], p["w_gate_t"], p["gb_col"], p["gb_row"],
              p["conv_w"], p["conv_b"], p["wq"], p["wk"], p["out_norm"], p["skip"],
              p["q_norm"], p["k_norm"], p["sink"], p["bd"], p["proj_a"], p["proj_b"],
              p["merge_w"], p["merge_b"], p["w_out"]]
    return pl.pallas_call(
        _mixer_kernel,
        grid=(batch, seq // ts),
        in_specs=[
            pl.BlockSpec((1, ts, D_MODEL), lambda b, j: (b, j, 0)),
            pl.BlockSpec((1, 9, D_MODEL), lambda b, j: (b, 0, 0)),
            pl.BlockSpec((1, ts, LANES), lambda b, j: (b, j, 0)),
        ] + [_const_spec(a.shape) for a in consts],
        out_specs=pl.BlockSpec((1, ts, D_MODEL), lambda b, j: (b, j, 0)),
        out_shape=jax.ShapeDtypeStruct(x.shape, x.dtype),
        scratch_shapes=[
            pltpu.VMEM((SUBLANES + ts, M_WIDTH), jnp.float32),
            pltpu.VMEM((M_HEADS, M_HEAD_DIM, M_HEAD_DIM), jnp.float32),
            pltpu.VMEM((M_HEADS, 1, M_HEAD_DIM), jnp.float32),
            pltpu.VMEM((M_HEADS, 1, LANES), jnp.float32),
            pltpu.VMEM((carry_rows + ts, A_KV_HEADS * LANES), jnp.bfloat16),
            pltpu.VMEM((A_KV_WIDTH, carry_rows + ts), jnp.bfloat16),
            pltpu.VMEM((ts, M_WIDTH), jnp.bfloat16),
            pltpu.VMEM((A_Q_WIDTH, ts), jnp.bfloat16),
        ],
        compiler_params=pltpu.CompilerParams(
            dimension_semantics=("arbitrary", "arbitrary"), vmem_limit_bytes=VMEM_LIMIT),
        name="mixer",
    )(x, mod, tab, *consts)


def _mixer_params(l, mix_norm, mix_w_in, m_gate_b, m_conv_w, m_conv_b, m_wq, m_wk, m_out_norm,
                  m_skip, a_q_norm, a_k_norm, a_sinks, proj_a, proj_b, merge_w, merge_b, w_out):
    w = mix_w_in[l]
    g0 = 3 * M_WIDTH
    g1 = g0 + 2 * M_HEADS
    w_gate = w[:, g0:g1]
    row = lambda v: v.reshape(1, -1)
    sink = jnp.repeat(a_sinks[l].reshape(A_KV_HEADS, A_GROUP), 2 * CHUNK, axis=1)
    head = jnp.arange(MXU_DIM) // A_HEAD_DIM
    return {
        "norm": row(mix_norm[l]),
        "w_main": _bf(jnp.concatenate([w[:, :g0], w[:, g1:g1 + A_Q_WIDTH + A_KV_WIDTH]], axis=1)),
        "w_v_t": _bf(w[:, g1 + A_Q_WIDTH + A_KV_WIDTH:].T),
        "w_gate": _bf(jnp.pad(w_gate, ((0, 0), (0, LANES - 2 * M_HEADS)))),
        "w_gate_t": _bf(jnp.pad(w_gate.T, ((0, 2 * SUBLANES - 2 * M_HEADS), (0, 0)))),
        "gb_col": jnp.pad(m_gate_b[l], (0, LANES - 2 * M_HEADS)).reshape(1, LANES),
        "gb_row": jnp.broadcast_to(
            jnp.pad(m_gate_b[l], (0, 2 * SUBLANES - 2 * M_HEADS))[:, None], (2 * SUBLANES, LANES)),
        "conv_w": m_conv_w[l],
        "conv_b": row(m_conv_b[l]),
        "wq": _bf(m_wq[l]),
        "wk": _bf(m_wk[l]),
        "out_norm": row(m_out_norm[l]),
        "skip": row(m_skip[l]),
        "q_norm": row(jnp.tile(a_q_norm[l], LANES // A_HEAD_DIM)),
        "k_norm": row(jnp.tile(a_k_norm[l], LANES // A_HEAD_DIM)),
        "sink": jnp.broadcast_to(sink[:, None, :], (A_KV_HEADS, SUBLANES, sink.shape[1])),
        "bd": _bf(head[:, None] == head[None, :]),
        "proj_a": _bf(proj_a[l]),
        "proj_b": _bf(proj_b[l]),
        "merge_w": _bf(merge_w[l]),
        "merge_b": row(merge_b[l]),
        "w_out": _bf(w_out[l]),
    }


def kernel(x, c, positions, ada_w, ada_b, ffn1_norm, ffn1_w_in, ffn1_w_out, mix_norm, mix_w_in, m_gate_b, m_conv_w, m_conv_b, m_wq, m_wk, m_out_norm, m_skip, a_q_norm, a_k_norm, a_sinks, proj_a, proj_b, merge_w, merge_b, w_out, ffn2_norm, ffn2_w_in, ffn2_w_out):
    batch = x.shape[0]
    mod_all = _ada_call(c, ada_w, ada_b).reshape(DEPTH, batch, 9, D_MODEL)
    tab = _rope_call(positions)
    for l in range(DEPTH):
        mod = mod_all[l]
        x = _ffn_call(x, mod, 0, ffn1_norm[l], _bf(ffn1_w_in[l]), _bf(ffn1_w_out[l]))
        x = _mixer_call(x, mod, tab, _mixer_params(
            l, mix_norm, mix_w_in, m_gate_b, m_conv_w, m_conv_b, m_wq, m_wk, m_out_norm, m_skip,
            a_q_norm, a_k_norm, a_sinks, proj_a, proj_b, merge_w, merge_b, w_out))
        x = _ffn_call(x, mod, 6, ffn2_norm[l], _bf(ffn2_w_in[l]), _bf(ffn2_w_out[l]))
    return x
```

```python
import functools
import math

import jax
import jax.numpy as jnp
from jax import lax
from jax.experimental import pallas as pl
from jax.experimental.pallas import tpu as pltpu

D_MODEL = 1024
DEPTH = 2
CHUNK = 64
M_HEADS = 4
M_HEAD_DIM = 256
M_WIDTH = M_HEADS * M_HEAD_DIM
M_CONV = 4
A_HEADS = 16
A_KV_HEADS = 4
A_GROUP = A_HEADS // A_KV_HEADS
A_HEAD_DIM = 64
A_Q_WIDTH = A_HEADS * A_HEAD_DIM
A_KV_WIDTH = A_KV_HEADS * A_HEAD_DIM
A_PREV_CHUNKS = 2
ROPE_THETA = 10000.0
D_FF = 2816
EPS = 1e-6

LANES = 128
SUBLANES = 8
MXU_DIM = 256

FFN_TILE = 1024
FFN_SUB = 512
FFN_CHUNK = 256
MIX_TILE = 256
ADA_BLOCK = 1536
VMEM_LIMIT = 56 * 1024 * 1024

LOG2_E = math.log2(math.e)
Q_SCALE_LOG2 = A_HEAD_DIM ** -0.5 * LOG2_E

_NT = (((1,), (1,)), ((), ()))
_TN = (((0,), (0,)), ((), ()))


def _bf(x):
    return x.astype(jnp.bfloat16)


def _dot(a, b):
    return jnp.dot(a, b, preferred_element_type=jnp.float32)


def _modulate(x, norm_w, shift, scale):
    var = jnp.mean(x * x, axis=-1, keepdims=True)
    y = x * lax.rsqrt(var + EPS)
    return (y * norm_w) * (1.0 + scale) + shift


def _sigmoid(x):
    return 1.0 / (1.0 + jnp.exp(-x))


def _log_sigmoid(x):
    return jnp.minimum(x, 0.0) - jnp.log(1.0 + jnp.exp(-jnp.abs(x)))


def _split3(x):
    hi = _bf(x)
    r = x - hi.astype(jnp.float32)
    mid = _bf(r)
    lo = _bf(r - mid.astype(jnp.float32))
    return hi, mid, lo


def _reduce_rows(x, op):
    while x.shape[0] > SUBLANES:
        h = x.shape[0] // 2
        x = op(x[:h], x[h:])
    return x


def _const_spec(shape):
    nd = len(shape)
    return pl.BlockSpec(shape, lambda *_: (0,) * nd, pipeline_mode=pl.Buffered(1))


def _ada_kernel(c_ref, w_ref, b_ref, o_ref):
    c = c_ref[...]
    c_act = c * _sigmoid(c)
    o_ref[0] = jnp.dot(c_act, w_ref[0], preferred_element_type=jnp.float32,
                       precision=lax.Precision.HIGHEST) + b_ref[0]


def _ada_call(c, ada_w, ada_b):
    batch = c.shape[0]
    n_out = ada_w.shape[-1]
    return pl.pallas_call(
        _ada_kernel,
        grid=(DEPTH, n_out // ADA_BLOCK),
        in_specs=[
            pl.BlockSpec((batch, D_MODEL), lambda l, j: (0, 0)),
            pl.BlockSpec((1, D_MODEL, ADA_BLOCK), lambda l, j: (l, 0, j)),
            pl.BlockSpec((1, 1, ADA_BLOCK), lambda l, j: (l, 0, j)),
        ],
        out_specs=pl.BlockSpec((1, batch, ADA_BLOCK), lambda l, j: (l, 0, j)),
        out_shape=jax.ShapeDtypeStruct((DEPTH, batch, n_out), jnp.float32),
        compiler_params=pltpu.CompilerParams(
            dimension_semantics=("arbitrary", "arbitrary"), vmem_limit_bytes=VMEM_LIMIT),
        name="adaln_mod",
    )(c, ada_w, ada_b.reshape(DEPTH, 1, n_out))


def _rope_kernel(pos_ref, freq_ref, o_ref):
    half = A_HEAD_DIM // 2
    ang = freq_ref[:, 0:1] * pos_ref[0].astype(jnp.float32)
    o_ref[0, 0:half, :] = jnp.cos(ang)
    o_ref[0, half:A_HEAD_DIM, :] = jnp.sin(ang)


def _rope_call(positions):
    batch, seq = positions.shape
    half = A_HEAD_DIM // 2
    inv_freq = ROPE_THETA ** (-jnp.arange(0, A_HEAD_DIM, 2, dtype=jnp.float32) / A_HEAD_DIM)
    freq_col = jnp.broadcast_to(inv_freq[:, None], (half, LANES))
    return pl.pallas_call(
        _rope_kernel,
        grid=(batch,),
        in_specs=[
            pl.BlockSpec((1, 1, seq), lambda b: (b, 0, 0)),
            pl.BlockSpec((half, LANES), lambda b: (0, 0)),
        ],
        out_specs=pl.BlockSpec((1, A_HEAD_DIM, seq), lambda b: (b, 0, 0)),
        out_shape=jax.ShapeDtypeStruct((batch, A_HEAD_DIM, seq), jnp.float32),
        compiler_params=pltpu.CompilerParams(
            dimension_semantics=("arbitrary",), vmem_limit_bytes=VMEM_LIMIT),
        name="rope_table",
    )(positions.reshape(batch, 1, seq), freq_col)


def _ffn_kernel(row0, x_ref, mod_ref, nw_ref, win_ref, wout_ref, o_ref, act_ref):
    mod = mod_ref[0]
    shift, scale, gate = mod[row0:row0 + 1], mod[row0 + 1:row0 + 2], mod[row0 + 2:row0 + 3]
    for r in range(FFN_TILE // FFN_SUB):
        rows = slice(r * FFN_SUB, (r + 1) * FFN_SUB)
        x = x_ref[0, rows, :]
        hb = _bf(_modulate(x, nw_ref[...], shift, scale))
        for c in range(D_FF // FFN_CHUNK):
            lo = c * FFN_CHUNK
            a = _dot(hb, win_ref[:, lo:lo + FFN_CHUNK])
            g = _dot(hb, win_ref[:, D_FF + lo:D_FF + lo + FFN_CHUNK])
            act_ref[rows, lo:lo + FFN_CHUNK] = _bf(a * _sigmoid(a) * g)
        y = _dot(act_ref[rows, :], wout_ref[...])
        o_ref[0, rows, :] = x + (0.5 * gate) * y


def _ffn_call(x, mod, row0, norm_w, w_in, w_out):
    batch, seq, _ = x.shape
    tile = FFN_TILE
    return pl.pallas_call(
        functools.partial(_ffn_kernel, row0),
        grid=(batch, seq // tile),
        in_specs=[
            pl.BlockSpec((1, tile, D_MODEL), lambda b, j: (b, j, 0)),
            pl.BlockSpec((1, 9, D_MODEL), lambda b, j: (b, 0, 0)),
            _const_spec((1, D_MODEL)),
            _const_spec((D_MODEL, 2 * D_FF)),
            _const_spec((D_FF, D_MODEL)),
        ],
        out_specs=pl.BlockSpec((1, tile, D_MODEL), lambda b, j: (b, j, 0)),
        out_shape=jax.ShapeDtypeStruct(x.shape, x.dtype),
        scratch_shapes=[pltpu.VMEM((tile, D_FF), jnp.bfloat16)],
        compiler_params=pltpu.CompilerParams(
            dimension_semantics=("arbitrary", "arbitrary"), vmem_limit_bytes=VMEM_LIMIT),
        name="ffn",
    )(x, mod, norm_w.reshape(1, D_MODEL), w_in, w_out)


def _mixer_kernel(x_ref, mod_ref, tab_ref, nw_ref, wmain_ref, wqkv_t_ref, wgate_ref, wgate_t_ref,
                  gb_col_ref, gb_row_ref, convw_ref, convb_ref, wq_ref, wk_ref,
                  onorm_ref, skip_ref, qn_ref, kn_ref, sink_ref,
                  proja_ref, projb_ref, mergew_ref, mergeb_ref, wout_ref,
                  o_ref,
                  uext_ref, c_ref, n_ref, m_ref, kext_ref, vext_ref, ya_ref, ybt_ref):
    j = pl.program_id(1)
    ts = MIX_TILE
    f32 = jnp.float32
    carry_rows = A_PREV_CHUNKS * CHUNK

    @pl.when(j == 0)
    def _reset():
        uext_ref[0:SUBLANES, :] = jnp.zeros((SUBLANES, M_WIDTH), f32)
        c_ref[...] = jnp.zeros(c_ref.shape, f32)
        n_ref[...] = jnp.zeros(n_ref.shape, f32)
        m_ref[...] = jnp.zeros(m_ref.shape, f32)
        kext_ref[:, 0:carry_rows] = jnp.zeros((kext_ref.shape[0], carry_rows), jnp.bfloat16)
        vext_ref[:, 0:carry_rows] = jnp.zeros((vext_ref.shape[0], carry_rows), jnp.bfloat16)

    x = x_ref[0]
    mod = mod_ref[0]
    hb = _bf(_modulate(x, nw_ref[...], mod[3:4], mod[4:5]))

    uext_ref[SUBLANES:SUBLANES + ts, :] = _dot(hb, wmain_ref[:, 0:M_WIDTH])
    uc = convb_ref[...] + convw_ref[M_CONV - 1:M_CONV, :] * uext_ref[SUBLANES:SUBLANES + ts, :]
    for k in range(1, M_CONV):
        uc = uc + convw_ref[M_CONV - 1 - k:M_CONV - k, :] * uext_ref[SUBLANES - k:SUBLANES - k + ts, :]
    uext_ref[0:SUBLANES, :] = uext_ref[ts:ts + SUBLANES, :]
    ua = uc * _sigmoid(uc)
    uab = _bf(ua)

    gcol = _dot(hb, wgate_ref[...]) + gb_col_ref[...]
    grow = lax.dot_general(wgate_t_ref[...], hb, _NT,
                           preferred_element_type=f32) + gb_row_ref[...][:, 0:1]
    rr = lax.broadcasted_iota(jnp.int32, (ts, ts), 0)
    cc = lax.broadcasted_iota(jnp.int32, (ts, ts), 1)
    causal = rr >= cc
    tri = jnp.where(causal, 1.0, 0.0).astype(jnp.bfloat16)
    tri_t = jnp.where(rr <= cc, 1.0, 0.0).astype(jnp.bfloat16)
    b_col = sum(_dot(tri, p) for p in _split3(_log_sigmoid(gcol)))
    b_row = sum(_dot(p, tri_t) for p in _split3(_log_sigmoid(grow)))

    for h in range(M_HEADS):
        lo = h * M_HEAD_DIM
        hi = lo + M_HEAD_DIM
        ua_h = uab[:, lo:hi]
        q = _dot(ua_h, wq_ref[h])
        k = _dot(ua_h, wk_ref[h]) * (M_HEAD_DIM ** -0.5)
        qb, kb = _bf(q), _bf(k)
        vb = _bf(_dot(hb, wmain_ref[:, M_WIDTH + lo:M_WIDTH + hi]))
        bt = b_col[:, M_HEADS + h:M_HEADS + h + 1]
        bs = b_row[M_HEADS + h:M_HEADS + h + 1, :]
        li_t = gcol[:, h:h + 1]
        li_s = grow[h:h + 1, :]
        m_prev = m_ref[h][:, 0:1]
        e_mat = jnp.where(causal, li_s - bs, -jnp.inf)
        r = jnp.maximum(m_prev, jnp.max(e_mat, axis=-1, keepdims=True))
        w_intra = jnp.exp(e_mat - r)
        w_inter = jnp.exp(m_prev - r)
        s = lax.dot_general(qb, kb, _NT, preferred_element_type=f32) * w_intra
        c_state = c_ref[h]
        n_state = n_ref[h]
        num = _dot(_bf(s), vb) + w_inter * _dot(qb, _bf(c_state))
        den = jnp.sum(s, axis=-1, keepdims=True) + w_inter * jnp.sum(q * n_state, axis=-1, keepdims=True)
        hh = num * (1.0 / jnp.maximum(jnp.abs(den), jnp.exp(-(bt + r))))
        b_last = bt[ts - 1:ts, :]
        g = b_last - bt + li_t
        a_end = b_last + m_prev
        m_new = jnp.maximum(a_end, jnp.max(g, axis=0, keepdims=True))
        wg = jnp.exp(g - m_new)
        decay = jnp.exp(a_end - m_new)
        kw = k * wg
        c_ref[h] = decay * c_state + lax.dot_general(_bf(kw), vb, _TN, preferred_element_type=f32)
        n_ref[h] = decay * n_state + jnp.sum(kw, axis=0, keepdims=True)
        m_ref[h] = jnp.broadcast_to(m_new, (1, LANES))
        hn = hh * lax.rsqrt(jnp.mean(hh * hh, axis=-1, keepdims=True) + EPS)
        o_pre = _dot(hb, wmain_ref[:, 2 * M_WIDTH + lo:2 * M_WIDTH + hi])
        ya = _sigmoid(o_pre) * (hn * onorm_ref[:, lo:hi] + skip_ref[:, lo:hi] * ua[:, lo:hi])
        ya_ref[:, lo:hi] = _bf(ya)

    half = A_HEAD_DIM // 2
    cos_t = tab_ref[0, 0:half, :]
    sin_t = tab_ref[0, half:A_HEAD_DIM, :]

    def norm_rope_t(t, nw):
        t = t * lax.rsqrt(jnp.mean(t * t, axis=0, keepdims=True) + EPS) * nw
        t1, t2 = t[0:half], t[half:A_HEAD_DIM]
        return jnp.concatenate([t1 * cos_t - t2 * sin_t, t2 * cos_t + t1 * sin_t], axis=0)

    qkv_t = lax.dot_general(wqkv_t_ref[...], hb, _NT, preferred_element_type=f32)
    k_row0 = A_Q_WIDTH
    v_row0 = A_Q_WIDTH + A_KV_WIDTH
    for g in range(A_KV_HEADS):
        rows = slice(k_row0 + g * A_HEAD_DIM, k_row0 + (g + 1) * A_HEAD_DIM)
        kext_ref[g * A_HEAD_DIM:(g + 1) * A_HEAD_DIM, carry_rows:carry_rows + ts] = _bf(
            norm_rope_t(qkv_t[rows], kn_ref[...]))
    vext_ref[:, carry_rows:carry_rows + ts] = _bf(qkv_t[v_row0:v_row0 + A_KV_WIDTH])

    pair = 2 * CHUNK
    win = pair + carry_rows
    second_chunk = (lax.broadcasted_iota(jnp.int32, (CHUNK, pair), 1) // CHUNK) % 2
    bias_lo = jnp.where(second_chunk == 1, -jnp.inf, 0.0)
    bias_hi = jnp.where(second_chunk == 1, 0.0, -jnp.inf)
    before_start = jnp.where(j == 0, -jnp.inf, 0.0)
    for g in range(A_KV_HEADS):
        q_heads = []
        for i in range(A_GROUP):
            row = (g * A_GROUP + i) * A_HEAD_DIM
            q_heads.append(_bf(norm_rope_t(qkv_t[row:row + A_HEAD_DIM], qn_ref[...]) * Q_SCALE_LOG2))
        sink = sink_ref[g][0:1, :] * LOG2_E
        head_rows = slice(g * A_HEAD_DIM, (g + 1) * A_HEAD_DIM)
        for p in range(ts // pair):
            r0 = p * pair
            q_blk = jnp.concatenate([t[:, r0:r0 + pair] for t in q_heads], axis=1)
            k_win = kext_ref[head_rows, r0:r0 + win]
            v_win = vext_ref[head_rows, r0:r0 + win]
            s = lax.dot_general(k_win, q_blk, _TN, preferred_element_type=f32)
            p_cols, inv_cols = [], []
            for i in range(A_GROUP):
                lanes = slice(i * pair, (i + 1) * pair)
                sk = [s[c * CHUNK:(c + 1) * CHUNK, lanes] for c in range(win // CHUNK)]
                sk[0] = sk[0] + (bias_lo + before_start if p == 0 else bias_lo)
                sk[-1] = sk[-1] + bias_hi
                if p == 0:
                    sk[1] = sk[1] + before_start
                m = jnp.maximum(jnp.maximum(sk[0], sk[1]), jnp.maximum(sk[2], sk[3]))
                m = jnp.maximum(jnp.max(_reduce_rows(m, jnp.maximum), axis=0, keepdims=True), sink[:, lanes])
                pk = [jnp.exp2(t - m) for t in sk]
                denom = jnp.sum(_reduce_rows((pk[0] + pk[1]) + (pk[2] + pk[3]), jnp.add), axis=0, keepdims=True)
                inv_cols.append(1.0 / (denom + jnp.exp2(sink[:, lanes] - m)))
                p_cols.append(_bf(jnp.concatenate(pk, axis=0)))
            o = _dot(v_win, jnp.concatenate(p_cols, axis=1)) * jnp.concatenate(inv_cols, axis=1)
            for i in range(A_GROUP):
                row = (g * A_GROUP + i) * A_HEAD_DIM
                ybt_ref[row:row + A_HEAD_DIM, r0:r0 + pair] = _bf(o[:, i * pair:(i + 1) * pair])
    kext_ref[:, 0:carry_rows] = kext_ref[:, ts:ts + carry_rows]
    vext_ref[:, 0:carry_rows] = vext_ref[:, ts:ts + carry_rows]

    gates = _sigmoid(_dot(hb, mergew_ref[...]) + mergeb_ref[...])
    merged = (gates[:, 0:D_MODEL] * _dot(ya_ref[...], proja_ref[...])
              + gates[:, D_MODEL:2 * D_MODEL]
              * lax.dot_general(ybt_ref[...], projb_ref[...], _TN, preferred_element_type=f32))
    y = _dot(_bf(merged), wout_ref[...])
    o_ref[0] = x + mod[5:6] * y


def _mixer_call(x, mod, tab, p):
    batch, seq, _ = x.shape
    ts = MIX_TILE
    carry_rows = A_PREV_CHUNKS * CHUNK
    consts = [p["norm"], p["w_main"], p["w_qkv_t"], p["w_gate"], p["w_gate_t"], p["gb_col"], p["gb_row"],
              p["conv_w"], p["conv_b"], p["wq"], p["wk"], p["out_norm"], p["skip"],
              p["q_norm"], p["k_norm"], p["sink"], p["proj_a"], p["proj_b"],
              p["merge_w"], p["merge_b"], p["w_out"]]
    return pl.pallas_call(
        _mixer_kernel,
        grid=(batch, seq // ts),
        in_specs=[
            pl.BlockSpec((1, ts, D_MODEL), lambda b, j: (b, j, 0)),
            pl.BlockSpec((1, 9, D_MODEL), lambda b, j: (b, 0, 0)),
            pl.BlockSpec((1, A_HEAD_DIM, ts), lambda b, j: (b, 0, j)),
        ] + [_const_spec(a.shape) for a in consts],
        out_specs=pl.BlockSpec((1, ts, D_MODEL), lambda b, j: (b, j, 0)),
        out_shape=jax.ShapeDtypeStruct(x.shape, x.dtype),
        scratch_shapes=[
            pltpu.VMEM((SUBLANES + ts, M_WIDTH), jnp.float32),
            pltpu.VMEM((M_HEADS, M_HEAD_DIM, M_HEAD_DIM), jnp.float32),
            pltpu.VMEM((M_HEADS, 1, M_HEAD_DIM), jnp.float32),
            pltpu.VMEM((M_HEADS, 1, LANES), jnp.float32),
            pltpu.VMEM((A_KV_WIDTH, carry_rows + ts), jnp.bfloat16),
            pltpu.VMEM((A_KV_WIDTH, carry_rows + ts), jnp.bfloat16),
            pltpu.VMEM((ts, M_WIDTH), jnp.bfloat16),
            pltpu.VMEM((A_Q_WIDTH, ts), jnp.bfloat16),
        ],
        compiler_params=pltpu.CompilerParams(
            dimension_semantics=("arbitrary", "arbitrary"), vmem_limit_bytes=VMEM_LIMIT),
        name="mixer",
    )(x, mod, tab, *consts)


def _mixer_params(l, mix_norm, mix_w_in, m_gate_b, m_conv_w, m_conv_b, m_wq, m_wk, m_out_norm,
                  m_skip, a_q_norm, a_k_norm, a_sinks, proj_a, proj_b, merge_w, merge_b, w_out):
    w = mix_w_in[l]
    g0 = 3 * M_WIDTH
    g1 = g0 + 2 * M_HEADS
    w_gate = w[:, g0:g1]
    row = lambda v: v.reshape(1, -1)
    sink = jnp.repeat(a_sinks[l].reshape(A_KV_HEADS, A_GROUP), 2 * CHUNK, axis=1)
    return {
        "norm": row(mix_norm[l]),
        "w_main": _bf(w[:, :g0]),
        "w_qkv_t": _bf(w[:, g1:].T),
        "w_gate": _bf(jnp.pad(w_gate, ((0, 0), (0, LANES - 2 * M_HEADS)))),
        "w_gate_t": _bf(jnp.pad(w_gate.T, ((0, 2 * SUBLANES - 2 * M_HEADS), (0, 0)))),
        "gb_col": jnp.pad(m_gate_b[l], (0, LANES - 2 * M_HEADS)).reshape(1, LANES),
        "gb_row": jnp.broadcast_to(
            jnp.pad(m_gate_b[l], (0, 2 * SUBLANES - 2 * M_HEADS))[:, None], (2 * SUBLANES, LANES)),
        "conv_w": m_conv_w[l],
        "conv_b": row(m_conv_b[l]),
        "wq": _bf(m_wq[l]),
        "wk": _bf(m_wk[l]),
        "out_norm": row(m_out_norm[l]),
        "skip": row(m_skip[l]),
        "q_norm": jnp.broadcast_to(a_q_norm[l][:, None], (A_HEAD_DIM, MIX_TILE)),
        "k_norm": jnp.broadcast_to(a_k_norm[l][:, None], (A_HEAD_DIM, MIX_TILE)),
        "sink": jnp.broadcast_to(sink[:, None, :], (A_KV_HEADS, SUBLANES, sink.shape[1])),
        "proj_a": _bf(proj_a[l]),
        "proj_b": _bf(proj_b[l]),
        "merge_w": _bf(merge_w[l]),
        "merge_b": row(merge_b[l]),
        "w_out": _bf(w_out[l]),
    }


def kernel(x, c, positions, ada_w, ada_b, ffn1_norm, ffn1_w_in, ffn1_w_out, mix_norm, mix_w_in, m_gate_b, m_conv_w, m_conv_b, m_wq, m_wk, m_out_norm, m_skip, a_q_norm, a_k_norm, a_sinks, proj_a, proj_b, merge_w, merge_b, w_out, ffn2_norm, ffn2_w_in, ffn2_w_out):
    batch = x.shape[0]
    mod_all = _ada_call(c, ada_w, ada_b).reshape(DEPTH, batch, 9, D_MODEL)
    tab = _rope_call(positions)
    for l in range(DEPTH):
        mod = mod_all[l]
        x = _ffn_call(x, mod, 0, ffn1_norm[l], _bf(ffn1_w_in[l]), _bf(ffn1_w_out[l]))
        x = _mixer_call(x, mod, tab, _mixer_params(
            l, mix_norm, mix_w_in, m_gate_b, m_conv_w, m_conv_b, m_wq, m_wk, m_out_norm, m_skip,
            a_q_norm, a_k_norm, a_sinks, proj_a, proj_b, merge_w, merge_b, w_out))
        x = _ffn_call(x, mod, 6, ffn2_norm[l], _bf(ffn2_w_in[l]), _bf(ffn2_w_out[l]))
    return x
```

```python
import functools
import math

import jax
import jax.numpy as jnp
from jax import lax
from jax.experimental import pallas as pl
from jax.experimental.pallas import tpu as pltpu

D_MODEL = 1024
DEPTH = 2
CHUNK = 64
M_HEADS = 4
M_HEAD_DIM = 256
M_WIDTH = M_HEADS * M_HEAD_DIM
M_CONV = 4
A_HEADS = 16
A_KV_HEADS = 4
A_GROUP = A_HEADS // A_KV_HEADS
A_HEAD_DIM = 64
A_Q_WIDTH = A_HEADS * A_HEAD_DIM
A_KV_WIDTH = A_KV_HEADS * A_HEAD_DIM
A_PREV_CHUNKS = 2
ROPE_THETA = 10000.0
D_FF = 2816
EPS = 1e-6

LANES = 128
SUBLANES = 8
MXU_DIM = 256

FFN_TILE = 1024
FFN_SUB = 512
FFN_CHUNK = 256
MIX_TILE = 256
ADA_BLOCK = 1536
VMEM_LIMIT = 56 * 1024 * 1024

LOG2_E = math.log2(math.e)
Q_SCALE_LOG2 = A_HEAD_DIM ** -0.5 * LOG2_E

_NT = (((1,), (1,)), ((), ()))
_TN = (((0,), (0,)), ((), ()))


def _bf(x):
    return x.astype(jnp.bfloat16)


def _dot(a, b):
    return jnp.dot(a, b, preferred_element_type=jnp.float32)


def _modulate(x, norm_w, shift, scale):
    var = jnp.mean(x * x, axis=-1, keepdims=True)
    y = x * lax.rsqrt(var + EPS)
    return (y * norm_w) * (1.0 + scale) + shift


def _sigmoid(x):
    return 1.0 / (1.0 + jnp.exp(-x))


def _log_sigmoid(x):
    return jnp.minimum(x, 0.0) - jnp.log(1.0 + jnp.exp(-jnp.abs(x)))


def _split3(x):
    hi = _bf(x)
    r = x - hi.astype(jnp.float32)
    mid = _bf(r)
    lo = _bf(r - mid.astype(jnp.float32))
    return hi, mid, lo


def _reduce_rows(x, op):
    while x.shape[0] > SUBLANES:
        h = x.shape[0] // 2
        x = op(x[:h], x[h:])
    return x


def _const_spec(shape):
    nd = len(shape)
    return pl.BlockSpec(shape, lambda *_: (0,) * nd, pipeline_mode=pl.Buffered(1))


def _ada_kernel(c_ref, w_ref, b_ref, o_ref):
    c = c_ref[...]
    c_act = c * _sigmoid(c)
    o_ref[0] = jnp.dot(c_act, w_ref[0], preferred_element_type=jnp.float32,
                       precision=lax.Precision.HIGHEST) + b_ref[0]


def _ada_call(c, ada_w, ada_b):
    batch = c.shape[0]
    n_out = ada_w.shape[-1]
    return pl.pallas_call(
        _ada_kernel,
        grid=(DEPTH, n_out // ADA_BLOCK),
        in_specs=[
            pl.BlockSpec((batch, D_MODEL), lambda l, j: (0, 0)),
            pl.BlockSpec((1, D_MODEL, ADA_BLOCK), lambda l, j: (l, 0, j)),
            pl.BlockSpec((1, 1, ADA_BLOCK), lambda l, j: (l, 0, j)),
        ],
        out_specs=pl.BlockSpec((1, batch, ADA_BLOCK), lambda l, j: (l, 0, j)),
        out_shape=jax.ShapeDtypeStruct((DEPTH, batch, n_out), jnp.float32),
        compiler_params=pltpu.CompilerParams(
            dimension_semantics=("arbitrary", "arbitrary"), vmem_limit_bytes=VMEM_LIMIT),
        name="adaln_mod",
    )(c, ada_w, ada_b.reshape(DEPTH, 1, n_out))


def _rope_kernel(pos_ref, freq_ref, o_ref):
    half = A_HEAD_DIM // 2
    ang = freq_ref[:, 0:1] * pos_ref[0].astype(jnp.float32)
    o_ref[0, 0:half, :] = jnp.cos(ang)
    o_ref[0, half:A_HEAD_DIM, :] = jnp.sin(ang)


def _rope_call(positions):
    batch, seq = positions.shape
    half = A_HEAD_DIM // 2
    inv_freq = ROPE_THETA ** (-jnp.arange(0, A_HEAD_DIM, 2, dtype=jnp.float32) / A_HEAD_DIM)
    freq_col = jnp.broadcast_to(inv_freq[:, None], (half, LANES))
    return pl.pallas_call(
        _rope_kernel,
        grid=(batch,),
        in_specs=[
            pl.BlockSpec((1, 1, seq), lambda b: (b, 0, 0)),
            pl.BlockSpec((half, LANES), lambda b: (0, 0)),
        ],
        out_specs=pl.BlockSpec((1, A_HEAD_DIM, seq), lambda b: (b, 0, 0)),
        out_shape=jax.ShapeDtypeStruct((batch, A_HEAD_DIM, seq), jnp.float32),
        compiler_params=pltpu.CompilerParams(
            dimension_semantics=("arbitrary",), vmem_limit_bytes=VMEM_LIMIT),
        name="rope_table",
    )(positions.reshape(batch, 1, seq), freq_col)


def _ffn_kernel(row0, x_ref, mod_ref, nw_ref, win_ref, wout_ref, o_ref, act_ref):
    mod = mod_ref[0]
    shift, scale, gate = mod[row0:row0 + 1], mod[row0 + 1:row0 + 2], mod[row0 + 2:row0 + 3]
    for r in range(FFN_TILE // FFN_SUB):
        rows = slice(r * FFN_SUB, (r + 1) * FFN_SUB)
        x = x_ref[0, rows, :]
        hb = _bf(_modulate(x, nw_ref[...], shift, scale))
        for c in range(D_FF // FFN_CHUNK):
            lo = c * FFN_CHUNK
            a = _dot(hb, win_ref[:, lo:lo + FFN_CHUNK])
            g = _dot(hb, win_ref[:, D_FF + lo:D_FF + lo + FFN_CHUNK])
            act_ref[rows, lo:lo + FFN_CHUNK] = _bf(a * _sigmoid(a) * g)
        y = _dot(act_ref[rows, :], wout_ref[...])
        o_ref[0, rows, :] = x + (0.5 * gate) * y


def _ffn_call(x, mod, row0, norm_w, w_in, w_out):
    batch, seq, _ = x.shape
    tile = FFN_TILE
    return pl.pallas_call(
        functools.partial(_ffn_kernel, row0),
        grid=(batch, seq // tile),
        in_specs=[
            pl.BlockSpec((1, tile, D_MODEL), lambda b, j: (b, j, 0)),
            pl.BlockSpec((1, 9, D_MODEL), lambda b, j: (b, 0, 0)),
            _const_spec((1, D_MODEL)),
            _const_spec((D_MODEL, 2 * D_FF)),
            _const_spec((D_FF, D_MODEL)),
        ],
        out_specs=pl.BlockSpec((1, tile, D_MODEL), lambda b, j: (b, j, 0)),
        out_shape=jax.ShapeDtypeStruct(x.shape, x.dtype),
        scratch_shapes=[pltpu.VMEM((tile, D_FF), jnp.bfloat16)],
        compiler_params=pltpu.CompilerParams(
            dimension_semantics=("arbitrary", "arbitrary"), vmem_limit_bytes=VMEM_LIMIT),
        name="ffn",
    )(x, mod, norm_w.reshape(1, D_MODEL), w_in, w_out)


def _mixer_kernel(x_ref, mod_ref, tab_ref, nw_ref, wmain_ref, wqkv_t_ref, wgate_ref, wgate_t_ref,
                  gb_col_ref, gb_row_ref, convw_ref, convb_ref, wq_ref, wk_ref,
                  onorm_ref, skip_ref, qn_ref, kn_ref, sink_ref,
                  proja_ref, projb_ref, mergew_ref, mergeb_ref, wout_ref,
                  xnext_ref, modnext_ref,
                  o_ref,
                  uext_ref, c_ref, n_ref, m_ref, kext_ref, vext_ref, ya_ref, ybt_ref, hb_ref):
    j = pl.program_id(1)
    ts = MIX_TILE
    f32 = jnp.float32
    carry_rows = A_PREV_CHUNKS * CHUNK

    def modulated(x_tile, mod_rows):
        return _bf(_modulate(x_tile, nw_ref[...], mod_rows[3:4], mod_rows[4:5]))

    @pl.when((pl.program_id(0) == 0) & (j == 0))
    def _first_tile():
        hb_ref[...] = modulated(x_ref[0], mod_ref[0])

    @pl.when(j == 0)
    def _reset():
        uext_ref[0:SUBLANES, :] = jnp.zeros((SUBLANES, M_WIDTH), f32)
        c_ref[...] = jnp.zeros(c_ref.shape, f32)
        n_ref[...] = jnp.zeros(n_ref.shape, f32)
        m_ref[...] = jnp.zeros(m_ref.shape, f32)
        kext_ref[:, 0:carry_rows] = jnp.zeros((kext_ref.shape[0], carry_rows), jnp.bfloat16)
        vext_ref[:, 0:carry_rows] = jnp.zeros((vext_ref.shape[0], carry_rows), jnp.bfloat16)

    x = x_ref[0]
    mod = mod_ref[0]
    hb = hb_ref[...]

    uext_ref[SUBLANES:SUBLANES + ts, :] = _dot(hb, wmain_ref[:, 0:M_WIDTH])
    uc = convb_ref[...] + convw_ref[M_CONV - 1:M_CONV, :] * uext_ref[SUBLANES:SUBLANES + ts, :]
    for k in range(1, M_CONV):
        uc = uc + convw_ref[M_CONV - 1 - k:M_CONV - k, :] * uext_ref[SUBLANES - k:SUBLANES - k + ts, :]
    uext_ref[0:SUBLANES, :] = uext_ref[ts:ts + SUBLANES, :]
    ua = uc * _sigmoid(uc)
    uab = _bf(ua)

    gcol = _dot(hb, wgate_ref[...]) + gb_col_ref[...]
    grow = lax.dot_general(wgate_t_ref[...], hb, _NT,
                           preferred_element_type=f32) + gb_row_ref[...][:, 0:1]
    rr = lax.broadcasted_iota(jnp.int32, (ts, ts), 0)
    cc = lax.broadcasted_iota(jnp.int32, (ts, ts), 1)
    causal = rr >= cc
    tri = jnp.where(causal, 1.0, 0.0).astype(jnp.bfloat16)
    tri_t = jnp.where(rr <= cc, 1.0, 0.0).astype(jnp.bfloat16)
    b_col = sum(_dot(tri, p) for p in _split3(_log_sigmoid(gcol)))
    b_row = sum(_dot(p, tri_t) for p in _split3(_log_sigmoid(grow)))

    for h in range(M_HEADS):
        lo = h * M_HEAD_DIM
        hi = lo + M_HEAD_DIM
        ua_h = uab[:, lo:hi]
        q = _dot(ua_h, wq_ref[h])
        k = _dot(ua_h, wk_ref[h]) * (M_HEAD_DIM ** -0.5)
        qb, kb = _bf(q), _bf(k)
        vb = _bf(_dot(hb, wmain_ref[:, M_WIDTH + lo:M_WIDTH + hi]))
        bt = b_col[:, M_HEADS + h:M_HEADS + h + 1]
        bs = b_row[M_HEADS + h:M_HEADS + h + 1, :]
        li_t = gcol[:, h:h + 1]
        li_s = grow[h:h + 1, :]
        m_prev = m_ref[h][:, 0:1]
        e_mat = jnp.where(causal, li_s - bs, -jnp.inf)
        r = jnp.maximum(m_prev, jnp.max(e_mat, axis=-1, keepdims=True))
        w_intra = jnp.exp(e_mat - r)
        w_inter = jnp.exp(m_prev - r)
        s = lax.dot_general(qb, kb, _NT, preferred_element_type=f32) * w_intra
        c_state = c_ref[h]
        n_state = n_ref[h]
        num = _dot(_bf(s), vb) + w_inter * _dot(qb, _bf(c_state))
        den = jnp.sum(s, axis=-1, keepdims=True) + w_inter * jnp.sum(q * n_state, axis=-1, keepdims=True)
        hh = num * (1.0 / jnp.maximum(jnp.abs(den), jnp.exp(-(bt + r))))
        b_last = bt[ts - 1:ts, :]
        g = b_last - bt + li_t
        a_end = b_last + m_prev
        m_new = jnp.maximum(a_end, jnp.max(g, axis=0, keepdims=True))
        wg = jnp.exp(g - m_new)
        decay = jnp.exp(a_end - m_new)
        kw = k * wg
        c_ref[h] = decay * c_state + lax.dot_general(_bf(kw), vb, _TN, preferred_element_type=f32)
        n_ref[h] = decay * n_state + jnp.sum(kw, axis=0, keepdims=True)
        m_ref[h] = jnp.broadcast_to(m_new, (1, LANES))
        hn = hh * lax.rsqrt(jnp.mean(hh * hh, axis=-1, keepdims=True) + EPS)
        o_pre = _dot(hb, wmain_ref[:, 2 * M_WIDTH + lo:2 * M_WIDTH + hi])
        ya = _sigmoid(o_pre) * (hn * onorm_ref[:, lo:hi] + skip_ref[:, lo:hi] * ua[:, lo:hi])
        ya_ref[:, lo:hi] = _bf(ya)

    half = A_HEAD_DIM // 2
    cos_t = tab_ref[0, 0:half, :]
    sin_t = tab_ref[0, half:A_HEAD_DIM, :]

    def norm_rope_t(t, nw):
        t = t * lax.rsqrt(jnp.mean(t * t, axis=0, keepdims=True) + EPS) * nw
        t1, t2 = t[0:half], t[half:A_HEAD_DIM]
        return jnp.concatenate([t1 * cos_t - t2 * sin_t, t2 * cos_t + t1 * sin_t], axis=0)

    qkv_t = lax.dot_general(wqkv_t_ref[...], hb, _NT, preferred_element_type=f32)
    k_row0 = A_Q_WIDTH
    v_row0 = A_Q_WIDTH + A_KV_WIDTH
    for g in range(A_KV_HEADS):
        rows = slice(k_row0 + g * A_HEAD_DIM, k_row0 + (g + 1) * A_HEAD_DIM)
        kext_ref[g * A_HEAD_DIM:(g + 1) * A_HEAD_DIM, carry_rows:carry_rows + ts] = _bf(
            norm_rope_t(qkv_t[rows], kn_ref[...]))
    vext_ref[:, carry_rows:carry_rows + ts] = _bf(qkv_t[v_row0:v_row0 + A_KV_WIDTH])

    pair = 2 * CHUNK
    win = pair + carry_rows
    second_chunk = (lax.broadcasted_iota(jnp.int32, (CHUNK, pair), 1) // CHUNK) % 2
    bias_lo = jnp.where(second_chunk == 1, -jnp.inf, 0.0)
    bias_hi = jnp.where(second_chunk == 1, 0.0, -jnp.inf)
    before_start = jnp.where(j == 0, -jnp.inf, 0.0)
    for g in range(A_KV_HEADS):
        q_heads = []
        for i in range(A_GROUP):
            row = (g * A_GROUP + i) * A_HEAD_DIM
            q_heads.append(_bf(norm_rope_t(qkv_t[row:row + A_HEAD_DIM], qn_ref[...]) * Q_SCALE_LOG2))
        sink = sink_ref[g][0:1, :] * LOG2_E
        head_rows = slice(g * A_HEAD_DIM, (g + 1) * A_HEAD_DIM)
        for p in range(ts // pair):
            r0 = p * pair
            q_blk = jnp.concatenate([t[:, r0:r0 + pair] for t in q_heads], axis=1)
            k_win = kext_ref[head_rows, r0:r0 + win]
            v_win = vext_ref[head_rows, r0:r0 + win]
            s = lax.dot_general(k_win, q_blk, _TN, preferred_element_type=f32)
            p_cols, inv_cols = [], []
            for i in range(A_GROUP):
                lanes = slice(i * pair, (i + 1) * pair)
                sk = [s[c * CHUNK:(c + 1) * CHUNK, lanes] for c in range(win // CHUNK)]
                sk[0] = sk[0] + (bias_lo + before_start if p == 0 else bias_lo)
                sk[-1] = sk[-1] + bias_hi
                if p == 0:
                    sk[1] = sk[1] + before_start
                m = jnp.maximum(jnp.maximum(sk[0], sk[1]), jnp.maximum(sk[2], sk[3]))
                m = jnp.maximum(jnp.max(_reduce_rows(m, jnp.maximum), axis=0, keepdims=True), sink[:, lanes])
                pk = [jnp.exp2(t - m) for t in sk]
                denom = jnp.sum(_reduce_rows((pk[0] + pk[1]) + (pk[2] + pk[3]), jnp.add), axis=0, keepdims=True)
                inv_cols.append(1.0 / (denom + jnp.exp2(sink[:, lanes] - m)))
                p_cols.append(_bf(jnp.concatenate(pk, axis=0)))
            o = _dot(v_win, jnp.concatenate(p_cols, axis=1)) * jnp.concatenate(inv_cols, axis=1)
            for i in range(A_GROUP):
                row = (g * A_GROUP + i) * A_HEAD_DIM
                ybt_ref[row:row + A_HEAD_DIM, r0:r0 + pair] = _bf(o[:, i * pair:(i + 1) * pair])
    kext_ref[:, 0:carry_rows] = kext_ref[:, ts:ts + carry_rows]
    vext_ref[:, 0:carry_rows] = vext_ref[:, ts:ts + carry_rows]

    gates = _sigmoid(_dot(hb, mergew_ref[...]) + mergeb_ref[...])
    merged = (gates[:, 0:D_MODEL] * _dot(ya_ref[...], proja_ref[...])
              + gates[:, D_MODEL:2 * D_MODEL]
              * lax.dot_general(ybt_ref[...], projb_ref[...], _TN, preferred_element_type=f32))
    y = _dot(_bf(merged), wout_ref[...])
    o_ref[0] = x + mod[5:6] * y
    hb_ref[...] = modulated(xnext_ref[0], modnext_ref[0])


def _mixer_call(x, mod, tab, p):
    batch, seq, _ = x.shape
    ts = MIX_TILE
    carry_rows = A_PREV_CHUNKS * CHUNK
    n_j = seq // ts

    def next_tile(b, j):
        flat = jnp.minimum(b * n_j + j + 1, batch * n_j - 1)
        return flat // n_j, flat % n_j

    consts = [p["norm"], p["w_main"], p["w_qkv_t"], p["w_gate"], p["w_gate_t"], p["gb_col"], p["gb_row"],
              p["conv_w"], p["conv_b"], p["wq"], p["wk"], p["out_norm"], p["skip"],
              p["q_norm"], p["k_norm"], p["sink"], p["proj_a"], p["proj_b"],
              p["merge_w"], p["merge_b"], p["w_out"]]
    return pl.pallas_call(
        _mixer_kernel,
        grid=(batch, seq // ts),
        in_specs=[
            pl.BlockSpec((1, ts, D_MODEL), lambda b, j: (b, j, 0)),
            pl.BlockSpec((1, 9, D_MODEL), lambda b, j: (b, 0, 0)),
            pl.BlockSpec((1, A_HEAD_DIM, ts), lambda b, j: (b, 0, j)),
        ] + [_const_spec(a.shape) for a in consts] + [
            pl.BlockSpec((1, ts, D_MODEL), lambda b, j: (*next_tile(b, j), 0)),
            pl.BlockSpec((1, 9, D_MODEL), lambda b, j: (next_tile(b, j)[0], 0, 0)),
        ],
        out_specs=pl.BlockSpec((1, ts, D_MODEL), lambda b, j: (b, j, 0)),
        out_shape=jax.ShapeDtypeStruct(x.shape, x.dtype),
        scratch_shapes=[
            pltpu.VMEM((SUBLANES + ts, M_WIDTH), jnp.float32),
            pltpu.VMEM((M_HEADS, M_HEAD_DIM, M_HEAD_DIM), jnp.float32),
            pltpu.VMEM((M_HEADS, 1, M_HEAD_DIM), jnp.float32),
            pltpu.VMEM((M_HEADS, 1, LANES), jnp.float32),
            pltpu.VMEM((A_KV_WIDTH, carry_rows + ts), jnp.bfloat16),
            pltpu.VMEM((A_KV_WIDTH, carry_rows + ts), jnp.bfloat16),
            pltpu.VMEM((ts, M_WIDTH), jnp.bfloat16),
            pltpu.VMEM((A_Q_WIDTH, ts), jnp.bfloat16),
            pltpu.VMEM((ts, D_MODEL), jnp.bfloat16),
        ],
        compiler_params=pltpu.CompilerParams(
            dimension_semantics=("arbitrary", "arbitrary"), vmem_limit_bytes=VMEM_LIMIT),
        name="mixer",
    )(x, mod, tab, *consts, x, mod)


def _mixer_params(l, mix_norm, mix_w_in, m_gate_b, m_conv_w, m_conv_b, m_wq, m_wk, m_out_norm,
                  m_skip, a_q_norm, a_k_norm, a_sinks, proj_a, proj_b, merge_w, merge_b, w_out):
    w = mix_w_in[l]
    g0 = 3 * M_WIDTH
    g1 = g0 + 2 * M_HEADS
    w_gate = w[:, g0:g1]
    row = lambda v: v.reshape(1, -1)
    sink = jnp.repeat(a_sinks[l].reshape(A_KV_HEADS, A_GROUP), 2 * CHUNK, axis=1)
    return {
        "norm": row(mix_norm[l]),
        "w_main": _bf(w[:, :g0]),
        "w_qkv_t": _bf(w[:, g1:].T),
        "w_gate": _bf(jnp.pad(w_gate, ((0, 0), (0, LANES - 2 * M_HEADS)))),
        "w_gate_t": _bf(jnp.pad(w_gate.T, ((0, 2 * SUBLANES - 2 * M_HEADS), (0, 0)))),
        "gb_col": jnp.pad(m_gate_b[l], (0, LANES - 2 * M_HEADS)).reshape(1, LANES),
        "gb_row": jnp.broadcast_to(
            jnp.pad(m_gate_b[l], (0, 2 * SUBLANES - 2 * M_HEADS))[:, None], (2 * SUBLANES, LANES)),
        "conv_w": m_conv_w[l],
        "conv_b": row(m_conv_b[l]),
        "wq": _bf(m_wq[l]),
        "wk": _bf(m_wk[l]),
        "out_norm": row(m_out_norm[l]),
        "skip": row(m_skip[l]),
        "q_norm": jnp.broadcast_to(a_q_norm[l][:, None], (A_HEAD_DIM, MIX_TILE)),
        "k_norm": jnp.broadcast_to(a_k_norm[l][:, None], (A_HEAD_DIM, MIX_TILE)),
        "sink": jnp.broadcast_to(sink[:, None, :], (A_KV_HEADS, SUBLANES, sink.shape[1])),
        "proj_a": _bf(proj_a[l]),
        "proj_b": _bf(proj_b[l]),
        "merge_w": _bf(merge_w[l]),
        "merge_b": row(merge_b[l]),
        "w_out": _bf(w_out[l]),
    }


def kernel(x, c, positions, ada_w, ada_b, ffn1_norm, ffn1_w_in, ffn1_w_out, mix_norm, mix_w_in, m_gate_b, m_conv_w, m_conv_b, m_wq, m_wk, m_out_norm, m_skip, a_q_norm, a_k_norm, a_sinks, proj_a, proj_b, merge_w, merge_b, w_out, ffn2_norm, ffn2_w_in, ffn2_w_out):
    batch = x.shape[0]
    mod_all = _ada_call(c, ada_w, ada_b).reshape(DEPTH, batch, 9, D_MODEL)
    tab = _rope_call(positions)
    for l in range(DEPTH):
        mod = mod_all[l]
        x = _ffn_call(x, mod, 0, ffn1_norm[l], _bf(ffn1_w_in[l]), _bf(ffn1_w_out[l]))
        x = _mixer_call(x, mod, tab, _mixer_params(
            l, mix_norm, mix_w_in, m_gate_b, m_conv_w, m_conv_b, m_wq, m_wk, m_out_norm, m_skip,
            a_q_norm, a_k_norm, a_sinks, proj_a, proj_b, merge_w, merge_b, w_out))
        x = _ffn_call(x, mod, 6, ffn2_norm[l], _bf(ffn2_w_in[l]), _bf(ffn2_w_out[l]))
    return x
```

```python
import functools
import math

import jax
import jax.numpy as jnp
from jax import lax
from jax.experimental import pallas as pl
from jax.experimental.pallas import tpu as pltpu

D_MODEL = 1024
DEPTH = 2
CHUNK = 64
M_HEADS = 4
M_HEAD_DIM = 256
M_WIDTH = M_HEADS * M_HEAD_DIM
M_CONV = 4
A_HEADS = 16
A_KV_HEADS = 4
A_GROUP = A_HEADS // A_KV_HEADS
A_HEAD_DIM = 64
A_Q_WIDTH = A_HEADS * A_HEAD_DIM
A_KV_WIDTH = A_KV_HEADS * A_HEAD_DIM
A_PREV_CHUNKS = 2
ROPE_THETA = 10000.0
D_FF = 2816
EPS = 1e-6

LANES = 128
SUBLANES = 8
MXU_DIM = 256

FFN_TILE = 1024
FFN_SUB = 512
FFN_CHUNK = 256
MIX_TILE = 256
ADA_BLOCK = 1536
VMEM_LIMIT = 56 * 1024 * 1024

LOG2_E = math.log2(math.e)
Q_SCALE_LOG2 = A_HEAD_DIM ** -0.5 * LOG2_E

_NT = (((1,), (1,)), ((), ()))
_TN = (((0,), (0,)), ((), ()))


def _bf(x):
    return x.astype(jnp.bfloat16)


def _dot(a, b):
    return jnp.dot(a, b, preferred_element_type=jnp.float32)


def _modulate(x, norm_w, shift, scale):
    var = jnp.mean(x * x, axis=-1, keepdims=True)
    y = x * lax.rsqrt(var + EPS)
    return (y * norm_w) * (1.0 + scale) + shift


def _sigmoid(x):
    return 1.0 / (1.0 + jnp.exp(-x))


def _log_sigmoid(x):
    return jnp.minimum(x, 0.0) - jnp.log(1.0 + jnp.exp(-jnp.abs(x)))


def _split3(x):
    hi = _bf(x)
    r = x - hi.astype(jnp.float32)
    mid = _bf(r)
    lo = _bf(r - mid.astype(jnp.float32))
    return hi, mid, lo


def _reduce_rows(x, op):
    while x.shape[0] > SUBLANES:
        h = x.shape[0] // 2
        x = op(x[:h], x[h:])
    return x


def _const_spec(shape):
    nd = len(shape)
    return pl.BlockSpec(shape, lambda *_: (0,) * nd, pipeline_mode=pl.Buffered(1))


def _ada_kernel(c_ref, w_ref, b_ref, o_ref):
    c = c_ref[...]
    c_act = c * _sigmoid(c)
    o_ref[0] = jnp.dot(c_act, w_ref[0], preferred_element_type=jnp.float32,
                       precision=lax.Precision.HIGHEST) + b_ref[0]


def _ada_call(c, ada_w, ada_b):
    batch = c.shape[0]
    n_out = ada_w.shape[-1]
    return pl.pallas_call(
        _ada_kernel,
        grid=(DEPTH, n_out // ADA_BLOCK),
        in_specs=[
            pl.BlockSpec((batch, D_MODEL), lambda l, j: (0, 0)),
            pl.BlockSpec((1, D_MODEL, ADA_BLOCK), lambda l, j: (l, 0, j)),
            pl.BlockSpec((1, 1, ADA_BLOCK), lambda l, j: (l, 0, j)),
        ],
        out_specs=pl.BlockSpec((1, batch, ADA_BLOCK), lambda l, j: (l, 0, j)),
        out_shape=jax.ShapeDtypeStruct((DEPTH, batch, n_out), jnp.float32),
        compiler_params=pltpu.CompilerParams(
            dimension_semantics=("arbitrary", "arbitrary"), vmem_limit_bytes=VMEM_LIMIT),
        name="adaln_mod",
    )(c, ada_w, ada_b.reshape(DEPTH, 1, n_out))


def _rope_kernel(pos_ref, freq_ref, o_ref):
    half = A_HEAD_DIM // 2
    ang = freq_ref[:, 0:1] * pos_ref[0].astype(jnp.float32)
    o_ref[0, 0:half, :] = jnp.cos(ang)
    o_ref[0, half:A_HEAD_DIM, :] = jnp.sin(ang)


def _rope_call(positions):
    batch, seq = positions.shape
    half = A_HEAD_DIM // 2
    inv_freq = ROPE_THETA ** (-jnp.arange(0, A_HEAD_DIM, 2, dtype=jnp.float32) / A_HEAD_DIM)
    freq_col = jnp.broadcast_to(inv_freq[:, None], (half, LANES))
    return pl.pallas_call(
        _rope_kernel,
        grid=(batch,),
        in_specs=[
            pl.BlockSpec((1, 1, seq), lambda b: (b, 0, 0)),
            pl.BlockSpec((half, LANES), lambda b: (0, 0)),
        ],
        out_specs=pl.BlockSpec((1, A_HEAD_DIM, seq), lambda b: (b, 0, 0)),
        out_shape=jax.ShapeDtypeStruct((batch, A_HEAD_DIM, seq), jnp.float32),
        compiler_params=pltpu.CompilerParams(
            dimension_semantics=("arbitrary",), vmem_limit_bytes=VMEM_LIMIT),
        name="rope_table",
    )(positions.reshape(batch, 1, seq), freq_col)


def _ffn_kernel(row0, x_ref, mod_ref, nw_ref, win_ref, wout_ref, o_ref, act_ref):
    mod = mod_ref[0]
    shift, scale, gate = mod[row0:row0 + 1], mod[row0 + 1:row0 + 2], mod[row0 + 2:row0 + 3]
    for r in range(FFN_TILE // FFN_SUB):
        rows = slice(r * FFN_SUB, (r + 1) * FFN_SUB)
        x = x_ref[0, rows, :]
        hb = _bf(_modulate(x, nw_ref[...], shift, scale))
        y = None
        for c in range(D_FF // FFN_CHUNK):
            lo = c * FFN_CHUNK
            a = _dot(hb, win_ref[:, lo:lo + FFN_CHUNK])
            g = _dot(hb, win_ref[:, D_FF + lo:D_FF + lo + FFN_CHUNK])
            part = _dot(_bf(a * _sigmoid(a) * g), wout_ref[lo:lo + FFN_CHUNK, :])
            y = part if y is None else y + part
        o_ref[0, rows, :] = x + (0.5 * gate) * y


def _ffn_call(x, mod, row0, norm_w, w_in, w_out):
    batch, seq, _ = x.shape
    tile = FFN_TILE
    return pl.pallas_call(
        functools.partial(_ffn_kernel, row0),
        grid=(batch, seq // tile),
        in_specs=[
            pl.BlockSpec((1, tile, D_MODEL), lambda b, j: (b, j, 0)),
            pl.BlockSpec((1, 9, D_MODEL), lambda b, j: (b, 0, 0)),
            _const_spec((1, D_MODEL)),
            _const_spec((D_MODEL, 2 * D_FF)),
            _const_spec((D_FF, D_MODEL)),
        ],
        out_specs=pl.BlockSpec((1, tile, D_MODEL), lambda b, j: (b, j, 0)),
        out_shape=jax.ShapeDtypeStruct(x.shape, x.dtype),
        scratch_shapes=[pltpu.VMEM((tile, D_FF), jnp.bfloat16)],
        compiler_params=pltpu.CompilerParams(
            dimension_semantics=("arbitrary", "arbitrary"), vmem_limit_bytes=VMEM_LIMIT),
        name="ffn",
    )(x, mod, norm_w.reshape(1, D_MODEL), w_in, w_out)


def _mixer_kernel(x_ref, mod_ref, tab_ref, nw_ref, wmain_ref, wqkv_t_ref, wgate_ref, wgate_t_ref,
                  gb_col_ref, gb_row_ref, convw_ref, convb_ref, wq_ref, wk_ref,
                  onorm_ref, skip_ref, qn_ref, kn_ref, sink_ref,
                  proja_ref, projb_ref, mergew_ref, mergeb_ref, wout_ref,
                  o_ref,
                  uext_ref, c_ref, n_ref, m_ref, kext_ref, vext_ref, ya_ref, ybt_ref):
    j = pl.program_id(1)
    ts = MIX_TILE
    f32 = jnp.float32
    carry_rows = A_PREV_CHUNKS * CHUNK

    @pl.when(j == 0)
    def _reset():
        uext_ref[0:SUBLANES, :] = jnp.zeros((SUBLANES, M_WIDTH), f32)
        c_ref[...] = jnp.zeros(c_ref.shape, f32)
        n_ref[...] = jnp.zeros(n_ref.shape, f32)
        m_ref[...] = jnp.zeros(m_ref.shape, f32)
        kext_ref[:, 0:carry_rows] = jnp.zeros((kext_ref.shape[0], carry_rows), jnp.bfloat16)
        vext_ref[:, 0:carry_rows] = jnp.zeros((vext_ref.shape[0], carry_rows), jnp.bfloat16)

    x = x_ref[0]
    mod = mod_ref[0]
    hb = _bf(_modulate(x, nw_ref[...], mod[3:4], mod[4:5]))

    uext_ref[SUBLANES:SUBLANES + ts, :] = _dot(hb, wmain_ref[:, 0:M_WIDTH])
    uc = convb_ref[...] + convw_ref[M_CONV - 1:M_CONV, :] * uext_ref[SUBLANES:SUBLANES + ts, :]
    for k in range(1, M_CONV):
        uc = uc + convw_ref[M_CONV - 1 - k:M_CONV - k, :] * uext_ref[SUBLANES - k:SUBLANES - k + ts, :]
    uext_ref[0:SUBLANES, :] = uext_ref[ts:ts + SUBLANES, :]
    ua = uc * _sigmoid(uc)
    uab = _bf(ua)

    gcol = _dot(hb, wgate_ref[...]) + gb_col_ref[...]
    grow = lax.dot_general(wgate_t_ref[...], hb, _NT,
                           preferred_element_type=f32) + gb_row_ref[...][:, 0:1]
    rr = lax.broadcasted_iota(jnp.int32, (ts, ts), 0)
    cc = lax.broadcasted_iota(jnp.int32, (ts, ts), 1)
    causal = rr >= cc
    tri = jnp.where(causal, 1.0, 0.0).astype(jnp.bfloat16)
    tri_t = jnp.where(rr <= cc, 1.0, 0.0).astype(jnp.bfloat16)
    b_col = sum(_dot(tri, p) for p in _split3(_log_sigmoid(gcol)))
    b_row = sum(_dot(p, tri_t) for p in _split3(_log_sigmoid(grow)))

    for h in range(M_HEADS):
        lo = h * M_HEAD_DIM
        hi = lo + M_HEAD_DIM
        ua_h = uab[:, lo:hi]
        q = _dot(ua_h, wq_ref[h])
        k = _dot(ua_h, wk_ref[h]) * (M_HEAD_DIM ** -0.5)
        qb, kb = _bf(q), _bf(k)
        vb = _bf(_dot(hb, wmain_ref[:, M_WIDTH + lo:M_WIDTH + hi]))
        bt = b_col[:, M_HEADS + h:M_HEADS + h + 1]
        bs = b_row[M_HEADS + h:M_HEADS + h + 1, :]
        li_t = gcol[:, h:h + 1]
        li_s = grow[h:h + 1, :]
        m_prev = m_ref[h][:, 0:1]
        e_mat = jnp.where(causal, li_s - bs, -jnp.inf)
        r = jnp.maximum(m_prev, jnp.max(e_mat, axis=-1, keepdims=True))
        w_intra = jnp.exp(e_mat - r)
        w_inter = jnp.exp(m_prev - r)
        s = lax.dot_general(qb, kb, _NT, preferred_element_type=f32) * w_intra
        c_state = c_ref[h]
        n_state = n_ref[h]
        num = _dot(_bf(s), vb) + w_inter * _dot(qb, _bf(c_state))
        den = jnp.sum(s, axis=-1, keepdims=True) + w_inter * jnp.sum(q * n_state, axis=-1, keepdims=True)
        hh = num * (1.0 / jnp.maximum(jnp.abs(den), jnp.exp(-(bt + r))))
        b_last = bt[ts - 1:ts, :]
        g = b_last - bt + li_t
        a_end = b_last + m_prev
        m_new = jnp.maximum(a_end, jnp.max(g, axis=0, keepdims=True))
        wg = jnp.exp(g - m_new)
        decay = jnp.exp(a_end - m_new)
        kw = k * wg
        c_ref[h] = decay * c_state + lax.dot_general(_bf(kw), vb, _TN, preferred_element_type=f32)
        n_ref[h] = decay * n_state + jnp.sum(kw, axis=0, keepdims=True)
        m_ref[h] = jnp.broadcast_to(m_new, (1, LANES))
        hn = hh * lax.rsqrt(jnp.mean(hh * hh, axis=-1, keepdims=True) + EPS)
        o_pre = _dot(hb, wmain_ref[:, 2 * M_WIDTH + lo:2 * M_WIDTH + hi])
        ya = _sigmoid(o_pre) * (hn * onorm_ref[:, lo:hi] + skip_ref[:, lo:hi] * ua[:, lo:hi])
        ya_ref[:, lo:hi] = _bf(ya)

    half = A_HEAD_DIM // 2
    cos_t = tab_ref[0, 0:half, :]
    sin_t = tab_ref[0, half:A_HEAD_DIM, :]

    def norm_rope_t(t, nw):
        t = t * lax.rsqrt(jnp.mean(t * t, axis=0, keepdims=True) + EPS) * nw
        t1, t2 = t[0:half], t[half:A_HEAD_DIM]
        return jnp.concatenate([t1 * cos_t - t2 * sin_t, t2 * cos_t + t1 * sin_t], axis=0)

    qkv_t = lax.dot_general(wqkv_t_ref[...], hb, _NT, preferred_element_type=f32)
    k_row0 = A_Q_WIDTH
    v_row0 = A_Q_WIDTH + A_KV_WIDTH
    for g in range(A_KV_HEADS):
        rows = slice(k_row0 + g * A_HEAD_DIM, k_row0 + (g + 1) * A_HEAD_DIM)
        kext_ref[g * A_HEAD_DIM:(g + 1) * A_HEAD_DIM, carry_rows:carry_rows + ts] = _bf(
            norm_rope_t(qkv_t[rows], kn_ref[...]))
    vext_ref[:, carry_rows:carry_rows + ts] = _bf(qkv_t[v_row0:v_row0 + A_KV_WIDTH])

    pair = 2 * CHUNK
    win = pair + carry_rows
    second_chunk = (lax.broadcasted_iota(jnp.int32, (CHUNK, pair), 1) // CHUNK) % 2
    bias_lo = jnp.where(second_chunk == 1, -jnp.inf, 0.0)
    bias_hi = jnp.where(second_chunk == 1, 0.0, -jnp.inf)
    before_start = jnp.where(j == 0, -jnp.inf, 0.0)
    for g in range(A_KV_HEADS):
        q_heads = []
        for i in range(A_GROUP):
            row = (g * A_GROUP + i) * A_HEAD_DIM
            q_heads.append(_bf(norm_rope_t(qkv_t[row:row + A_HEAD_DIM], qn_ref[...]) * Q_SCALE_LOG2))
        sink = sink_ref[g][0:1, :] * LOG2_E
        head_rows = slice(g * A_HEAD_DIM, (g + 1) * A_HEAD_DIM)
        for p in range(ts // pair):
            r0 = p * pair
            q_blk = jnp.concatenate([t[:, r0:r0 + pair] for t in q_heads], axis=1)
            k_win = kext_ref[head_rows, r0:r0 + win]
            v_win = vext_ref[head_rows, r0:r0 + win]
            s = lax.dot_general(k_win, q_blk, _TN, preferred_element_type=f32)
            p_cols, inv_cols = [], []
            for i in range(A_GROUP):
                lanes = slice(i * pair, (i + 1) * pair)
                sk = [s[c * CHUNK:(c + 1) * CHUNK, lanes] for c in range(win // CHUNK)]
                sk[0] = sk[0] + (bias_lo + before_start if p == 0 else bias_lo)
                sk[-1] = sk[-1] + bias_hi
                if p == 0:
                    sk[1] = sk[1] + before_start
                m = jnp.maximum(jnp.maximum(sk[0], sk[1]), jnp.maximum(sk[2], sk[3]))
                m = jnp.maximum(jnp.max(_reduce_rows(m, jnp.maximum), axis=0, keepdims=True), sink[:, lanes])
                pk = [jnp.exp2(t - m) for t in sk]
                denom = jnp.sum(_reduce_rows((pk[0] + pk[1]) + (pk[2] + pk[3]), jnp.add), axis=0, keepdims=True)
                inv_cols.append(1.0 / (denom + jnp.exp2(sink[:, lanes] - m)))
                p_cols.append(_bf(jnp.concatenate(pk, axis=0)))
            o = _dot(v_win, jnp.concatenate(p_cols, axis=1)) * jnp.concatenate(inv_cols, axis=1)
            for i in range(A_GROUP):
                row = (g * A_GROUP + i) * A_HEAD_DIM
                ybt_ref[row:row + A_HEAD_DIM, r0:r0 + pair] = _bf(o[:, i * pair:(i + 1) * pair])
    kext_ref[:, 0:carry_rows] = kext_ref[:, ts:ts + carry_rows]
    vext_ref[:, 0:carry_rows] = vext_ref[:, ts:ts + carry_rows]

    gates = _sigmoid(_dot(hb, mergew_ref[...]) + mergeb_ref[...])
    merged = (gates[:, 0:D_MODEL] * _dot(ya_ref[...], proja_ref[...])
              + gates[:, D_MODEL:2 * D_MODEL]
              * lax.dot_general(ybt_ref[...], projb_ref[...], _TN, preferred_element_type=f32))
    y = _dot(_bf(merged), wout_ref[...])
    o_ref[0] = x + mod[5:6] * y


def _mixer_call(x, mod, tab, p):
    batch, seq, _ = x.shape
    ts = MIX_TILE
    carry_rows = A_PREV_CHUNKS * CHUNK
    consts = [p["norm"], p["w_main"], p["w_qkv_t"], p["w_gate"], p["w_gate_t"], p["gb_col"], p["gb_row"],
              p["conv_w"], p["conv_b"], p["wq"], p["wk"], p["out_norm"], p["skip"],
              p["q_norm"], p["k_norm"], p["sink"], p["proj_a"], p["proj_b"],
              p["merge_w"], p["merge_b"], p["w_out"]]
    return pl.pallas_call(
        _mixer_kernel,
        grid=(batch, seq // ts),
        in_specs=[
            pl.BlockSpec((1, ts, D_MODEL), lambda b, j: (b, j, 0)),
            pl.BlockSpec((1, 9, D_MODEL), lambda b, j: (b, 0, 0)),
            pl.BlockSpec((1, A_HEAD_DIM, ts), lambda b, j: (b, 0, j)),
        ] + [_const_spec(a.shape) for a in consts],
        out_specs=pl.BlockSpec((1, ts, D_MODEL), lambda b, j: (b, j, 0)),
        out_shape=jax.ShapeDtypeStruct(x.shape, x.dtype),
        scratch_shapes=[
            pltpu.VMEM((SUBLANES + ts, M_WIDTH), jnp.float32),
            pltpu.VMEM((M_HEADS, M_HEAD_DIM, M_HEAD_DIM), jnp.float32),
            pltpu.VMEM((M_HEADS, 1, M_HEAD_DIM), jnp.float32),
            pltpu.VMEM((M_HEADS, 1, LANES), jnp.float32),
            pltpu.VMEM((A_KV_WIDTH, carry_rows + ts), jnp.bfloat16),
            pltpu.VMEM((A_KV_WIDTH, carry_rows + ts), jnp.bfloat16),
            pltpu.VMEM((ts, M_WIDTH), jnp.bfloat16),
            pltpu.VMEM((A_Q_WIDTH, ts), jnp.bfloat16),
        ],
        compiler_params=pltpu.CompilerParams(
            dimension_semantics=("arbitrary", "arbitrary"), vmem_limit_bytes=VMEM_LIMIT),
        name="mixer",
    )(x, mod, tab, *consts)


def _mixer_params(l, mix_norm, mix_w_in, m_gate_b, m_conv_w, m_conv_b, m_wq, m_wk, m_out_norm,
                  m_skip, a_q_norm, a_k_norm, a_sinks, proj_a, proj_b, merge_w, merge_b, w_out):
    w = mix_w_in[l]
    g0 = 3 * M_WIDTH
    g1 = g0 + 2 * M_HEADS
    w_gate = w[:, g0:g1]
    row = lambda v: v.reshape(1, -1)
    sink = jnp.repeat(a_sinks[l].reshape(A_KV_HEADS, A_GROUP), 2 * CHUNK, axis=1)
    return {
        "norm": row(mix_norm[l]),
        "w_main": _bf(w[:, :g0]),
        "w_qkv_t": _bf(w[:, g1:].T),
        "w_gate": _bf(jnp.pad(w_gate, ((0, 0), (0, LANES - 2 * M_HEADS)))),
        "w_gate_t": _bf(jnp.pad(w_gate.T, ((0, 2 * SUBLANES - 2 * M_HEADS), (0, 0)))),
        "gb_col": jnp.pad(m_gate_b[l], (0, LANES - 2 * M_HEADS)).reshape(1, LANES),
        "gb_row": jnp.broadcast_to(
            jnp.pad(m_gate_b[l], (0, 2 * SUBLANES - 2 * M_HEADS))[:, None], (2 * SUBLANES, LANES)),
        "conv_w": m_conv_w[l],
        "conv_b": row(m_conv_b[l]),
        "wq": _bf(m_wq[l]),
        "wk": _bf(m_wk[l]),
        "out_norm": row(m_out_norm[l]),
        "skip": row(m_skip[l]),
        "q_norm": jnp.broadcast_to(a_q_norm[l][:, None], (A_HEAD_DIM, MIX_TILE)),
        "k_norm": jnp.broadcast_to(a_k_norm[l][:, None], (A_HEAD_DIM, MIX_TILE)),
        "sink": jnp.broadcast_to(sink[:, None, :], (A_KV_HEADS, SUBLANES, sink.shape[1])),
        "proj_a": _bf(proj_a[l]),
        "proj_b": _bf(proj_b[l]),
        "merge_w": _bf(merge_w[l]),
        "merge_b": row(merge_b[l]),
        "w_out": _bf(w_out[l]),
    }


def kernel(x, c, positions, ada_w, ada_b, ffn1_norm, ffn1_w_in, ffn1_w_out, mix_norm, mix_w_in, m_gate_b, m_conv_w, m_conv_b, m_wq, m_wk, m_out_norm, m_skip, a_q_norm, a_k_norm, a_sinks, proj_a, proj_b, merge_w, merge_b, w_out, ffn2_norm, ffn2_w_in, ffn2_w_out):
    batch = x.shape[0]
    mod_all = _ada_call(c, ada_w, ada_b).reshape(DEPTH, batch, 9, D_MODEL)
    tab = _rope_call(positions)
    for l in range(DEPTH):
        mod = mod_all[l]
        x = _ffn_call(x, mod, 0, ffn1_norm[l], _bf(ffn1_w_in[l]), _bf(ffn1_w_out[l]))
        x = _mixer_call(x, mod, tab, _mixer_params(
            l, mix_norm, mix_w_in, m_gate_b, m_conv_w, m_conv_b, m_wq, m_wk, m_out_norm, m_skip,
            a_q_norm, a_k_norm, a_sinks, proj_a, proj_b, merge_w, merge_b, w_out))
        x = _ffn_call(x, mod, 6, ffn2_norm[l], _bf(ffn2_w_in[l]), _bf(ffn2_w_out[l]))
    return x
```
